```python
import jax
import jax.numpy as jnp
from jax import lax
import numpy as np

D_MODEL = 2048
BATCH = 16
SEQ = 256
DEPTH = 2
DEC_BATCH = 2
DEC_SEQ = 4096
PAST_LEN = 256

GRID_W = 64
N_EVEN = (DEPTH + 1) // 2
N_ODD = DEPTH // 2
N_MOD = 9
D_FF = 5632
RMS_EPS = 1e-6
CHUNK = 64
MLSTM_HEADS = 4
MLSTM_DQK = 128
MLSTM_DV = 256
MLSTM_CONV_W = 3
GLA_HEADS = 4
GLA_DK = 128
GLA_DV = 256
GLA_RANK = 16
GLA_GATE_NORM = 16.0
MIX_W = MLSTM_HEADS * MLSTM_DV + GLA_HEADS * GLA_DV
AB_SIZES = (2 * MLSTM_HEADS * MLSTM_DQK, MLSTM_HEADS * MLSTM_DV, MLSTM_HEADS * MLSTM_DV, 4 * MLSTM_HEADS,
            GLA_HEADS * GLA_DK, GLA_HEADS * GLA_DK, GLA_HEADS * GLA_DV, GLA_HEADS * GLA_DV, 2 * GLA_RANK)
AB_COLS = sum(AB_SIZES)
HEAD_DIM = 128
N_Q_HEADS = D_MODEL // HEAD_DIM
N_KV_HEADS = 4
Q_PER_KV = N_Q_HEADS // N_KV_HEADS
QKV_COLS = (N_Q_HEADS + 2 * N_KV_HEADS) * HEAD_DIM
ROPE_AXIS = HEAD_DIM // 2
ROPE_THETA = 10000.0
Q_BLOCK = 128

kernel_name = 'hybrid_mlstm_gla_gqa_diffusion_step'

F32 = jnp.float32


def rms_unit(x):
    xf = x.astype(F32)
    return xf * lax.rsqrt(jnp.mean(xf * xf, axis=-1, keepdims=True) + RMS_EPS)


def rmsnorm(x, g):
    return (rms_unit(x) * g.astype(F32)).astype(x.dtype)


def modulation(cvec, w, b):
    m = jax.nn.silu(cvec) @ w + b
    return m.reshape(cvec.shape[0], 1, N_MOD, D_MODEL)


def adaln_in(x, g, mod, j):
    return rmsnorm(x, g) * (1.0 + mod[:, :, 3 * j + 1]) + mod[:, :, 3 * j]


def swiglu(h, wg, wu, wd):
    return (jax.nn.silu(h @ wg) * (h @ wu)) @ wd


def macaron_half(x, g, mod, j, wg, wu, wd):
    return x + 0.5 * mod[:, :, 3 * j + 2] * swiglu(adaln_in(x, g, mod, j), wg, wu, wd)


def centred_depthwise_conv(x, w, b):
    width, ch = w.shape
    y = lax.conv_general_dilated(x, w.astype(x.dtype)[:, None, :], window_strides=(1,),
                                 padding=[(width // 2, width // 2)],
                                 dimension_numbers=('NWC', 'WIO', 'NWC'), feature_group_count=ch)
    return y + b.astype(x.dtype)


def to_chunks(a):
    b, s = a.shape[:2]
    a = a.reshape((b, s // CHUNK, CHUNK) + a.shape[2:])
    return a.transpose((1, 0, 3, 2) + tuple(range(4, a.ndim)))


def from_chunks(o):
    o = o.transpose((1, 0, 3, 2) + tuple(range(4, o.ndim)))
    return o.reshape((o.shape[0], o.shape[1] * o.shape[2]) + o.shape[3:])


def mlstm_scan(q, k, v, ig, lf, c0, n0, m0):
    causal = jnp.tril(jnp.ones((CHUNK, CHUNK), dtype=bool))

    def body(carry, inp):
        cm, nv, m = carry
        qc, kc, vc, ic, fc = inp
        b = jnp.cumsum(fc, axis=-1)
        dmat = jnp.where(causal, b[..., :, None] - b[..., None, :] + ic[..., None, :], -jnp.inf)
        inter = b + m[..., None]
        m_t = jnp.maximum(inter, jnp.max(dmat, axis=-1))
        w_intra = jnp.exp(dmat - m_t[..., None])
        w_inter = jnp.exp(inter - m_t)
        s = jnp.einsum('bhtd,bhsd->bhts', qc, kc) * w_intra
        num = w_inter[..., None] * jnp.einsum('bhtd,bhde->bhte', qc, cm) + jnp.einsum('bhts,bhse->bhte', s, vc)
        den = w_inter * jnp.einsum('bhtd,bhd->bht', qc, nv) + jnp.sum(s, axis=-1)
        h = num / jnp.maximum(jnp.abs(den), jnp.exp(-m_t))[..., None]
        g_end = b[..., -1]
        dec = g_end[..., None] - b + ic
        m_new = jnp.maximum(g_end + m, jnp.max(dec, axis=-1))
        ws = jnp.exp(dec - m_new[..., None])
        wc = jnp.exp(g_end + m - m_new)
        c_new = wc[..., None, None] * cm + jnp.einsum('bhs,bhsd,bhse->bhde', ws, kc, vc)
        n_new = wc[..., None] * nv + jnp.einsum('bhs,bhsd->bhd', ws, kc)
        return (c_new, n_new, m_new), h

    xs = (to_chunks(q), to_chunks(k), to_chunks(v), to_chunks(ig), to_chunks(lf))
    (cf, nf, mf), hs = lax.scan(body, (c0, n0, m0), xs)
    return from_chunks(hs), (cf, nf, mf)


def gla_scan(q, k, v, la, s0):
    causal = jnp.tril(jnp.ones((CHUNK, CHUNK), dtype=bool))

    def body(st, inp):
        qc, kc, vc, ac = inp
        bc = jnp.cumsum(ac, axis=2)
        o_inter = jnp.einsum('bhtd,bhde->bhte', qc * jnp.exp(bc), st)
        diff = bc[:, :, :, None, :] - bc[:, :, None, :, :]
        decay = jnp.exp(jnp.where(causal[:, :, None], diff, -jnp.inf))
        a = jnp.einsum('bhtd,bhsd,bhtsd->bhts', qc, kc, decay)
        o = o_inter + jnp.einsum('bhts,bhse->bhte', a, vc)
        b_end = bc[:, :, -1:, :]
        s_new = jnp.exp(b_end[:, :, 0, :])[..., None] * st + jnp.einsum('bhsd,bhse->bhde', kc * jnp.exp(b_end - bc), vc)
        return s_new, o

    sf, os_ = lax.scan(body, s0, (to_chunks(q), to_chunks(k), to_chunks(v), to_chunks(la)))
    return from_chunks(os_), (sf,)


def bidirectional(scan_fn, shared, per_dir, init):
    outs, finals = [], []
    for d in range(2):
        prep = (lambda a: jnp.flip(a, axis=1)) if d == 1 else (lambda a: a)
        args = [prep(a) for a in shared] + [prep(a[:, :, d]) for a in per_dir]
        out, fin = scan_fn(*args, *[s[:, d] for s in init])
        outs.append(prep(out))
        finals.append(fin)
    stacked = tuple(jnp.stack([f0, f1], axis=1) for f0, f1 in zip(finals[0], finals[1]))
    return outs[0] + outs[1], stacked


def mixer_ab(h, w_in, conv_w, conv_b, b_i, b_f, a_out_g, gk_w, gk_b, b_out_g, w_out, c0, n0, m0, s0):
    bsz, s, _ = h.shape
    proj = (h @ w_in).astype(F32)
    a_qk, a_v, a_o, a_g, b_q, b_k, b_v, b_g, b_lr = jnp.split(proj, np.cumsum(AB_SIZES)[:-1].tolist(), axis=-1)
    hqk = MLSTM_HEADS * MLSTM_DQK
    qk = jax.nn.silu(centred_depthwise_conv(a_qk, conv_w, conv_b))
    mq = qk[..., :hqk].reshape(bsz, s, MLSTM_HEADS, MLSTM_DQK)
    mk = qk[..., hqk:].reshape(bsz, s, MLSTM_HEADS, MLSTM_DQK) * (MLSTM_DQK ** -0.5)
    mv = a_v.reshape(bsz, s, MLSTM_HEADS, MLSTM_DV)
    gates = a_g.reshape(bsz, s, 2, 2, MLSTM_HEADS)
    ig = gates[:, :, :, 0] + b_i.astype(F32)
    lf = jax.nn.log_sigmoid(gates[:, :, :, 1] + b_f.astype(F32))
    hm, (cf, nf, mf) = bidirectional(mlstm_scan, (mq, mk, mv), (ig, lf), (c0, n0, m0))
    ya = rms_unit(hm).reshape(bsz, s, -1) * a_out_g * jax.nn.sigmoid(a_o)
    gq = b_q.reshape(bsz, s, GLA_HEADS, GLA_DK) * (GLA_DK ** -0.5)
    gkk = b_k.reshape(bsz, s, GLA_HEADS, GLA_DK)
    gv = b_v.reshape(bsz, s, GLA_HEADS, GLA_DV)
    lr = b_lr.reshape(bsz, s, 2, GLA_RANK)
    glogit = jnp.einsum('bsjr,jrn->bsjn', lr, gk_w.astype(F32)) + gk_b.astype(F32)
    la = (jax.nn.log_sigmoid(glogit) / GLA_GATE_NORM).reshape(bsz, s, 2, GLA_HEADS, GLA_DK)
    hg, (sf,) = bidirectional(gla_scan, (gq, gkk, gv), (la,), (s0,))
    yb = rms_unit(hg).reshape(bsz, s, -1) * b_out_g * jax.nn.silu(b_g)
    y = jnp.concatenate([ya, yb], axis=-1).astype(h.dtype) @ w_out
    return y, (cf, nf, mf, sf)


def axial_rope_tables(n_tokens):
    rows = n_tokens // GRID_W
    t = jnp.arange(rows * GRID_W)
    row = (t // GRID_W).astype(F32)
    col = (t % GRID_W).astype(F32)
    inv = ROPE_THETA ** (-jnp.arange(0, ROPE_AXIS, 2, dtype=F32) / ROPE_AXIS)
    ang = jnp.stack([row[:, None] * inv, col[:, None] * inv], axis=1)
    return jnp.cos(ang), jnp.sin(ang)


def apply_axial_rope(x, cos, sin):
    half = ROPE_AXIS // 2
    xf = x.astype(F32).reshape(x.shape[:-1] + (2, 2, half))
    shape = (x.shape[1],) + (1,) * (x.ndim - 3) + (2, half)
    cs, sn = cos.reshape(shape), sin.reshape(shape)
    x1, x2 = xf[..., 0, :], xf[..., 1, :]
    out = jnp.stack([x1 * cs - x2 * sn, x1 * sn + x2 * cs], axis=-2)
    return out.reshape(x.shape).astype(x.dtype)


def qk_normed_qkv(h, w_qkv, qg, kg):
    bsz, s, _ = h.shape
    proj = h @ w_qkv
    nq, nk = N_Q_HEADS * HEAD_DIM, N_KV_HEADS * HEAD_DIM
    q = proj[..., :nq].reshape(bsz, s, N_KV_HEADS, Q_PER_KV, HEAD_DIM)
    k = proj[..., nq:nq + nk].reshape(bsz, s, N_KV_HEADS, HEAD_DIM)
    v = proj[..., nq + nk:].reshape(bsz, s, N_KV_HEADS, HEAD_DIM)
    q = (rms_unit(q) * qg.astype(F32)).astype(h.dtype)
    k = (rms_unit(k) * kg.astype(F32)).astype(h.dtype)
    return q, k, v


def blocked_attention(q, k, v):
    bsz, sq, kvh, g, hd = q.shape
    qb = jnp.moveaxis(q.reshape(bsz, sq // Q_BLOCK, Q_BLOCK, kvh, g, hd), 1, 0)

    def one_block(qi):
        sc = jnp.einsum('bqkgd,bskd->bkgqs', qi, k).astype(F32) * (hd ** -0.5)
        p = jax.nn.softmax(sc, axis=-1).astype(v.dtype)
        return jnp.einsum('bkgqs,bskd->bqkgd', p, v)

    o = lax.map(one_block, qb)
    return jnp.moveaxis(o, 0, 1).reshape(bsz, sq, kvh * g * hd)


def attn_context(h, w_qkv, qg, kg, w_o):
    q, k, v = qk_normed_qkv(h, w_qkv, qg, kg)
    return blocked_attention(q, k, v) @ w_o, k, v


def attn_latent(h, ck, cv, w_qkv, qg, kg, w_o, cos, sin):
    q, k, v = qk_normed_qkv(h, w_qkv, qg, kg)
    q = apply_axial_rope(q, cos, sin)
    k = apply_axial_rope(k, cos, sin)
    k_all = jnp.concatenate([ck.astype(k.dtype), k], axis=1)
    v_all = jnp.concatenate([cv.astype(v.dtype), v], axis=1)
    return blocked_attention(q, k_all, v_all) @ w_o


def setup_inputs(seed: int = 0) -> dict:
    key = jax.random.key(seed)
    ks = iter(jax.random.split(key, 40))

    def nrm(shape, scale):
        return jax.random.normal(next(ks), shape, F32) * scale

    D = D_MODEL
    return {
        'x_prompt': nrm((BATCH, SEQ, D), 1.0),
        'x_sample': nrm((DEC_BATCH, DEC_SEQ, D), 1.0),
        'c': nrm((DEC_BATCH, D), 1.0),
        'c_ctx': nrm((D,), 1.0),
        'state_mlstm_C': nrm((DEC_BATCH, N_EVEN, 2, MLSTM_HEADS, MLSTM_DQK, MLSTM_DV), 0.3),
        'state_mlstm_n': nrm((DEC_BATCH, N_EVEN, 2, MLSTM_HEADS, MLSTM_DQK), 0.3),
        'state_mlstm_m': nrm((DEC_BATCH, N_EVEN, 2, MLSTM_HEADS), 1.0),
        'state_gla_S': nrm((DEC_BATCH, N_EVEN, 2, GLA_HEADS, GLA_DK, GLA_DV), 1.0),
        'cache_k': nrm((DEC_BATCH, N_ODD, PAST_LEN, N_KV_HEADS, HEAD_DIM), 1.0),
        'cache_v': nrm((DEC_BATCH, N_ODD, PAST_LEN, N_KV_HEADS, HEAD_DIM), 1.0),
        'w_mod': nrm((DEPTH, D, N_MOD * D), 0.5 * D ** -0.5),
        'b_mod': nrm((DEPTH, N_MOD * D), 0.02),
        'norm_g': 1.0 + nrm((DEPTH, 3, D), 0.02),
        'ffn_w_gate': nrm((DEPTH, 2, D, D_FF), D ** -0.5),
        'ffn_w_up': nrm((DEPTH, 2, D, D_FF), D ** -0.5),
        'ffn_w_down': nrm((DEPTH, 2, D_FF, D), D_FF ** -0.5),
        'w_in_ab': nrm((N_EVEN, D, AB_COLS), D ** -0.5),
        'mlstm_conv_w': nrm((N_EVEN, MLSTM_CONV_W, 2 * MLSTM_HEADS * MLSTM_DQK), MLSTM_CONV_W ** -0.5),
        'mlstm_conv_b': nrm((N_EVEN, 2 * MLSTM_HEADS * MLSTM_DQK), 0.02),
        'mlstm_b_i': nrm((N_EVEN, 2, MLSTM_HEADS), 0.1),
        'mlstm_b_f': 3.0 + nrm((N_EVEN, 2, MLSTM_HEADS), 0.5),
        'mlstm_out_g': 1.0 + nrm((N_EVEN, MLSTM_HEADS * MLSTM_DV), 0.02),
        'gla_w_gk': nrm((N_EVEN, 2, GLA_RANK, GLA_HEADS * GLA_DK), GLA_RANK ** -0.5),
        'gla_b_gk': nrm((N_EVEN, 2, GLA_HEADS * GLA_DK), 0.1),
        'gla_out_g': 1.0 + nrm((N_EVEN, GLA_HEADS * GLA_DV), 0.02),
        'w_out_ab': nrm((N_EVEN, MIX_W, D), MIX_W ** -0.5),
        'w_qkv': nrm((N_ODD, D, QKV_COLS), D ** -0.5),
        'q_norm_g': 1.0 + nrm((N_ODD, HEAD_DIM), 0.02),
        'k_norm_g': 1.0 + nrm((N_ODD, HEAD_DIM), 0.02),
        'w_o': nrm((N_ODD, N_Q_HEADS * HEAD_DIM, D), (N_Q_HEADS * HEAD_DIM) ** -0.5),
    }


def reference(x_prompt, x_sample, c, c_ctx, state_mlstm_C, state_mlstm_n, state_mlstm_m, state_gla_S,
              cache_k, cache_v, w_mod, b_mod, norm_g, ffn_w_gate, ffn_w_up, ffn_w_down, w_in_ab,
              mlstm_conv_w, mlstm_conv_b, mlstm_b_i, mlstm_b_f, mlstm_out_g, gla_w_gk, gla_b_gk, gla_out_g,
              w_out_ab, w_qkv, q_norm_g, k_norm_g, w_o):
    bp = x_prompt.shape[0]
    rope_cos, rope_sin = axial_rope_tables(x_sample.shape[1])
    xp, xs = x_prompt, x_sample
    new_c, new_n, new_m, new_s, new_k, new_v = [], [], [], [], [], []
    for l in range(DEPTH):
        mod_p = modulation(c_ctx[None, :], w_mod[l], b_mod[l])
        mod_s = modulation(c, w_mod[l], b_mod[l])
        xp = macaron_half(xp, norm_g[l, 0], mod_p, 0, ffn_w_gate[l, 0], ffn_w_up[l, 0], ffn_w_down[l, 0])
        xs = macaron_half(xs, norm_g[l, 0], mod_s, 0, ffn_w_gate[l, 0], ffn_w_up[l, 0], ffn_w_down[l, 0])
        hp = adaln_in(xp, norm_g[l, 1], mod_p, 1)
        hs = adaln_in(xs, norm_g[l, 1], mod_s, 1)
        if l % 2 == 0:
            e = l // 2
            wts = (w_in_ab[e], mlstm_conv_w[e], mlstm_conv_b[e], mlstm_b_i[e], mlstm_b_f[e], mlstm_out_g[e],
                   gla_w_gk[e], gla_b_gk[e], gla_out_g[e], w_out_ab[e])
            c0 = jnp.zeros((bp, 2, MLSTM_HEADS, MLSTM_DQK, MLSTM_DV), F32)
            n0 = jnp.zeros((bp, 2, MLSTM_HEADS, MLSTM_DQK), F32)
            m0 = jnp.zeros((bp, 2, MLSTM_HEADS), F32)
            s0 = jnp.zeros((bp, 2, GLA_HEADS, GLA_DK, GLA_DV), F32)
            yp, (cf, nf, mf, sf) = mixer_ab(hp, *wts, c0, n0, m0, s0)
            ys, _ = mixer_ab(hs, *wts, state_mlstm_C[:, e].astype(F32), state_mlstm_n[:, e].astype(F32),
                             state_mlstm_m[:, e].astype(F32), state_gla_S[:, e].astype(F32))
            new_c.append(cf)
            new_n.append(nf)
            new_m.append(mf)
            new_s.append(sf)
        else:
            o = l // 2
            yp, kp, vp = attn_context(hp, w_qkv[o], q_norm_g[o], k_norm_g[o], w_o[o])
            ys = attn_latent(hs, cache_k[:, o], cache_v[:, o], w_qkv[o], q_norm_g[o], k_norm_g[o], w_o[o],
                             rope_cos, rope_sin)
            new_k.append(kp)
            new_v.append(vp)
        xp = xp + mod_p[:, :, 5] * yp
        xs = xs + mod_s[:, :, 5] * ys
        xp = macaron_half(xp, norm_g[l, 2], mod_p, 2, ffn_w_gate[l, 1], ffn_w_up[l, 1], ffn_w_down[l, 1])
        xs = macaron_half(xs, norm_g[l, 2], mod_s, 2, ffn_w_gate[l, 1], ffn_w_up[l, 1], ffn_w_down[l, 1])
    return (xp, xs, jnp.stack(new_c, axis=1), jnp.stack(new_n, axis=1), jnp.stack(new_m, axis=1),
            jnp.stack(new_s, axis=1), jnp.stack(new_k, axis=1), jnp.stack(new_v, axis=1))
```

```python
import functools

import jax
import jax.numpy as jnp
from jax import lax
from jax.experimental import pallas as pl
from jax.experimental.pallas import tpu as pltpu

F32 = jnp.float32
BF16 = jnp.bfloat16
HIGHEST = lax.Precision.HIGHEST

D_MODEL = 2048
N_MOD = 9
D_FF = 5632
RMS_EPS = 1e-6
ROWS_PER_MOD = 4096
HEADS = 4
DQK = 128
DV = 256
GLA_RANK = 16
GLA_GATE_NORM = 16.0
CONV_W = 3
N_Q_HEADS = 16
Q_PER_KV = 4
HEAD_DIM = 128
GRID_W = 64
ROPE_AXIS = HEAD_DIM // 2
ROPE_THETA = 10000.0

COL_AQ, COL_AK, COL_AV, COL_AO = 0, 512, 1024, 2048
COL_BQ, COL_BK, COL_BV, COL_BG = 3072, 3584, 4096, 5120
COL_SMALL = 6144
AB_PAD_COLS = 6400
LR_OFF = 16

VMEM_LIMIT = 56 * 1024 * 1024
SCAN_L = 128
FFN_TM, FFN_TF = 512, 512
PROJ_TM = 512
OUT_TM = 256
PREP_TR = 256
ATT_TQ = 128


def _cparams(sem):
    return pltpu.CompilerParams(dimension_semantics=sem, vmem_limit_bytes=VMEM_LIMIT)


def _log_sigmoid(x):
    return jnp.minimum(x, 0.0) - jnp.log1p(jnp.exp(-jnp.abs(x)))


def _adaln(x, g, shift, scale):
    ms = jnp.mean(x * x, axis=-1, keepdims=True)
    return (x * lax.rsqrt(ms + RMS_EPS) * g) * (1.0 + scale) + shift


def _mod_index(tm):
    return lambda i, *_: ((i * tm) // ROWS_PER_MOD, 0, 0)


def _mod_kernel(c_ref, w_ref, b_ref, o_ref):
    c = c_ref[...]
    s = (c * jax.nn.sigmoid(c)).astype(BF16)
    o_ref[...] = jnp.dot(s, w_ref[...].astype(BF16), preferred_element_type=F32) + b_ref[...]


def _modulation(c8, w_mod, b_mod):
    depth, d, n = w_mod.shape
    tn = 1024
    return pl.pallas_call(
        _mod_kernel,
        grid=(depth, n // tn),
        in_specs=[
            pl.BlockSpec((8, d), lambda l, j: (0, 0)),
            pl.BlockSpec((None, d, tn), lambda l, j: (l, 0, j)),
            pl.BlockSpec((None, 1, tn), lambda l, j: (l, 0, j)),
        ],
        out_specs=pl.BlockSpec((None, 8, tn), lambda l, j: (l, 0, j)),
        out_shape=jax.ShapeDtypeStruct((depth, 8, n), F32),
        compiler_params=_cparams(("arbitrary", "arbitrary")),
        name="modulation",
    )(c8, w_mod, b_mod.reshape(depth, 1, n))


def _ffn_kernel(x_ref, mod_ref, g_ref, wg_ref, wu_ref, wd_ref, o_ref, h_ref, *, nf):
    f = pl.program_id(1)

    @pl.when(f == 0)
    def _():
        h = _adaln(x_ref[...], g_ref[...], mod_ref[0:1, :], mod_ref[1:2, :])
        h_ref[...] = h.astype(BF16)
        o_ref[...] = jnp.zeros_like(o_ref)

    h = h_ref[...]
    gate = jnp.dot(h, wg_ref[...], preferred_element_type=F32)
    up = jnp.dot(h, wu_ref[...], preferred_element_type=F32)
    a = (gate * jax.nn.sigmoid(gate) * up).astype(BF16)
    o_ref[...] += jnp.dot(a, wd_ref[...], preferred_element_type=F32)

    @pl.when(f == nf - 1)
    def _():
        o_ref[...] = x_ref[...] + (0.5 * mod_ref[2:3, :]) * o_ref[...]


def _ffn(x, mod3, g, wg, wu, wd):
    m, d = x.shape
    tm, tf = FFN_TM, FFN_TF
    nf = D_FF // tf
    return pl.pallas_call(
        functools.partial(_ffn_kernel, nf=nf),
        grid=(m // tm, nf),
        in_specs=[
            pl.BlockSpec((tm, d), lambda i, f: (i, 0)),
            pl.BlockSpec((None, 3, d), _mod_index(tm)),
            pl.BlockSpec((1, d), lambda i, f: (0, 0)),
            pl.BlockSpec((d, tf), lambda i, f: (0, f)),
            pl.BlockSpec((d, tf), lambda i, f: (0, f)),
            pl.BlockSpec((tf, d), lambda i, f: (f, 0)),
        ],
        out_specs=pl.BlockSpec((tm, d), lambda i, f: (i, 0)),
        out_shape=jax.ShapeDtypeStruct((m, d), F32),
        scratch_shapes=[pltpu.VMEM((tm, d), BF16)],
        compiler_params=_cparams(("parallel", "arbitrary")),
        name="ffn",
    )(x, mod3, g, wg, wu, wd)


def _proj_kernel(x_ref, mod_ref, g_ref, w_ref, o_ref, h_ref):
    @pl.when(pl.program_id(1) == 0)
    def _():
        h = _adaln(x_ref[...], g_ref[...], mod_ref[0:1, :], mod_ref[1:2, :])
        h_ref[...] = h.astype(BF16)

    o_ref[...] = jnp.dot(h_ref[...], w_ref[...], preferred_element_type=F32)


def _proj(x, mod3, g, w, tn):
    m, d = x.shape
    n = w.shape[1]
    tm = PROJ_TM
    return pl.pallas_call(
        _proj_kernel,
        grid=(m // tm, n // tn),
        in_specs=[
            pl.BlockSpec((tm, d), lambda i, j: (i, 0)),
            pl.BlockSpec((None, 3, d), _mod_index(tm)),
            pl.BlockSpec((1, d), lambda i, j: (0, 0)),
            pl.BlockSpec((d, tn), lambda i, j: (0, j)),
        ],
        out_specs=pl.BlockSpec((tm, tn), lambda i, j: (i, j)),
        out_shape=jax.ShapeDtypeStruct((m, n), F32),
        scratch_shapes=[pltpu.VMEM((tm, d), BF16)],
        compiler_params=_cparams(("parallel", "arbitrary")),
        name="adaln_proj",
    )(x, mod3, g, w)


def _conv_kernel(x_ref, prev_ref, next_ref, w_ref, b_ref, q_ref, k_ref, *, tr, seq):
    i = pl.program_id(0)
    x = x_ref[...]
    first = (i * tr) % seq == 0
    last = ((i + 1) * tr) % seq == 0
    prev_row = jnp.where(first, 0.0, prev_ref[7:8, :])
    next_row = jnp.where(last, 0.0, next_ref[0:1, :])
    ridx = lax.broadcasted_iota(jnp.int32, x.shape, 0)
    xm = jnp.where(ridx == 0, prev_row, pltpu.roll(x, 1, 0))
    xp = jnp.where(ridx == tr - 1, next_row, pltpu.roll(x, tr - 1, 0))
    y = xm * w_ref[0:1, :] + x * w_ref[1:2, :] + xp * w_ref[2:3, :] + b_ref[...]
    y = y * jax.nn.sigmoid(y)
    hqk = HEADS * DQK
    q_ref[...] = y[:, :hqk].astype(BF16)
    k_ref[...] = (y[:, hqk:] * (DQK ** -0.5)).astype(BF16)


def _conv_qk(proj, conv_w, conv_b, seq):
    m = proj.shape[0]
    tr = PREP_TR
    c = 2 * HEADS * DQK
    nb8 = m // 8
    return pl.pallas_call(
        functools.partial(_conv_kernel, tr=tr, seq=seq),
        grid=(m // tr,),
        in_specs=[
            pl.BlockSpec((tr, c), lambda i: (i, 0)),
            pl.BlockSpec((8, c), lambda i: (jnp.maximum(i * (tr // 8) - 1, 0), 0)),
            pl.BlockSpec((8, c), lambda i: (jnp.minimum((i + 1) * (tr // 8), nb8 - 1), 0)),
            pl.BlockSpec((CONV_W, c), lambda i: (0, 0)),
            pl.BlockSpec((1, c), lambda i: (0, 0)),
        ],
        out_specs=[
            pl.BlockSpec((tr, c // 2), lambda i: (i, 0)),
            pl.BlockSpec((tr, c // 2), lambda i: (i, 0)),
        ],
        out_shape=[jax.ShapeDtypeStruct((m, c // 2), BF16)] * 2,
        compiler_params=_cparams(("parallel",)),
        name="mlstm_conv",
    )(proj, proj, proj, conv_w, conv_b)


def _order_mask(n, rev):
    t = lax.broadcasted_iota(jnp.int32, (n, n), 0)
    s = lax.broadcasted_iota(jnp.int32, (n, n), 1)
    return (s >= t) if rev else (s <= t)


def _lane_select(cols):
    r = lax.broadcasted_iota(jnp.int32, (128, 128), 0)
    return jnp.concatenate([(r == c).astype(F32) for c in cols], axis=1)


def _dot_exact(a, b):
    return jnp.dot(a, b, precision=HIGHEST, preferred_element_type=F32)


def _dot_nt(a, b):
    return lax.dot_general(a, b, (((1,), (1,)), ((), ())), preferred_element_type=F32)


def _dot_tn(a, b):
    return lax.dot_general(a, b, (((0,), (0,)), ((), ())), preferred_element_type=F32)


def _wide(x):
    return jnp.concatenate([x, x], axis=1)


def _mlstm_kernel(q_ref, k_ref, v_ref, g_ref, gb_ref, c0_ref, n0_ref, m0_ref, *rest,
                  nc, rev, direction, emit_state):
    if emit_state:
        h_ref, cf_ref, nf_ref, mf_ref, c_ref, n_ref, m_ref = rest
    else:
        h_ref, c_ref, n_ref, m_ref = rest
    L = SCAN_L
    c = pl.program_id(1)

    @pl.when(c == 0)
    def _():
        c_ref[...] = c0_ref[...]
        n_ref[...] = n0_ref[...]
        m_ref[...] = m0_ref[...]

    mask = _order_mask(L, rev)
    gates = g_ref[...] + gb_ref[...]
    logf = _log_sigmoid(gates)
    bsum = _dot_exact(mask.astype(F32), logf)
    col_i = [8 * direction + h for h in range(HEADS)]
    col_f = [8 * direction + HEADS + h for h in range(HEADS)]
    i_rep = _dot_exact(gates, _lane_select(col_i))
    b_rep = _dot_exact(bsum, _lane_select(col_f))
    gates_t = gates.T
    bsum_t = bsum.T
    end = 0 if rev else L - 1

    for h in range(HEADS):
        hs = slice(h * DQK, (h + 1) * DQK)
        vs = slice(h * DV, (h + 1) * DV)
        b_c = b_rep[:, hs]
        i_c = i_rep[:, hs]
        b_r = bsum_t[col_f[h]:col_f[h] + 1, :]
        i_r = gates_t[col_i[h]:col_i[h] + 1, :]
        m_prev = m_ref[h:h + 1, :]
        dmat = jnp.where(mask, b_c - b_r + i_r, -jnp.inf)
        inter = b_c + m_prev
        m_t = jnp.maximum(inter, jnp.max(dmat, axis=-1, keepdims=True))
        w_intra = jnp.exp(dmat - m_t)
        w_inter = jnp.exp(inter - m_t)
        qh = q_ref[:, hs]
        kh = k_ref[:, hs]
        vh = v_ref[:, vs].astype(BF16)
        s = _dot_nt(qh, kh) * w_intra
        c_old = c_ref[h]
        num = _wide(w_inter) * jnp.dot(qh, c_old.astype(BF16), preferred_element_type=F32)
        num = num + jnp.dot(s.astype(BF16), vh, preferred_element_type=F32)
        qn = jnp.sum(qh.astype(F32) * n_ref[h:h + 1, :], axis=-1, keepdims=True)
        den = w_inter * qn + jnp.sum(s, axis=-1, keepdims=True)
        inv = 1.0 / jnp.maximum(jnp.abs(den), jnp.exp(-m_t))
        h_ref[:, vs] = num * _wide(inv)

        g_end = b_c[end:end + 1, :]
        dec = g_end - b_c + i_c
        m_new = jnp.maximum(g_end + m_prev, jnp.max(dec, axis=0, keepdims=True))
        ws = jnp.exp(dec - m_new)
        wc = jnp.exp(g_end + m_prev - m_new)
        kw = kh.astype(F32) * ws
        c_ref[h] = _wide(wc) * c_old + _dot_tn(kw.astype(BF16), vh)
        n_ref[h:h + 1, :] = wc * n_ref[h:h + 1, :] + jnp.sum(kw, axis=0, keepdims=True)
        m_ref[h:h + 1, :] = m_new

    if emit_state:
        @pl.when(c == nc - 1)
        def _():
            cf_ref[...] = c_ref[...]
            nf_ref[...] = n_ref[...]
            mf_ref[...] = m_ref[...]


def _mlstm_scan(q, k, proj, gbias, c0, n0, m0, *, batch, seq, direction, emit_state):
    L = SCAN_L
    nc = seq // L
    rev = direction == 1
    m = batch * seq

    def row(b, c):
        return b * nc + (nc - 1 - c if rev else c)

    in_specs = [
        pl.BlockSpec((L, HEADS * DQK), lambda b, c: (row(b, c), 0)),
        pl.BlockSpec((L, HEADS * DQK), lambda b, c: (row(b, c), 0)),
        pl.BlockSpec((L, HEADS * DV), lambda b, c: (row(b, c), COL_AV // (HEADS * DV))),
        pl.BlockSpec((L, 128), lambda b, c: (row(b, c), COL_SMALL // 128)),
        pl.BlockSpec((1, 128), lambda b, c: (0, 0)),
        pl.BlockSpec((None, None, HEADS, DQK, DV), lambda b, c: (b, direction, 0, 0, 0)),
        pl.BlockSpec((None, None, HEADS, DQK), lambda b, c: (b, direction, 0, 0)),
        pl.BlockSpec((None, None, HEADS, 128), lambda b, c: (b, direction, 0, 0)),
    ]
    out_specs = [pl.BlockSpec((L, HEADS * DV), lambda b, c: (row(b, c), 0))]
    out_shape = [jax.ShapeDtypeStruct((m, HEADS * DV), F32)]
    if emit_state:
        out_specs += [
            pl.BlockSpec((None, HEADS, DQK, DV), lambda b, c: (b, 0, 0, 0)),
            pl.BlockSpec((None, HEADS, DQK), lambda b, c: (b, 0, 0)),
            pl.BlockSpec((None, HEADS, 128), lambda b, c: (b, 0, 0)),
        ]
        out_shape += [
            jax.ShapeDtypeStruct((batch, HEADS, DQK, DV), F32),
            jax.ShapeDtypeStruct((batch, HEADS, DQK), F32),
            jax.ShapeDtypeStruct((batch, HEADS, 128), F32),
        ]
    return pl.pallas_call(
        functools.partial(_mlstm_kernel, nc=nc, rev=rev, direction=direction, emit_state=emit_state),
        grid=(batch, nc),
        in_specs=in_specs,
        out_specs=out_specs,
        out_shape=out_shape,
        scratch_shapes=[
            pltpu.VMEM((HEADS, DQK, DV), F32),
            pltpu.VMEM((HEADS, DQK), F32),
            pltpu.VMEM((HEADS, 128), F32),
        ],
        compiler_params=_cparams(("parallel", "arbitrary")),
        name="mlstm_scan",
    )(q, k, proj, proj, gbias, c0, n0, m0)


def _gla_kernel(q_ref, k_ref, v_ref, g_ref, wgk_ref, bgk_ref, s0_ref, *rest, nc, rev, emit_state):
    if emit_state:
        o_ref, sf_ref, s_ref = rest
    else:
        o_ref, s_ref = rest
    L = SCAN_L
    H = L // 2
    c = pl.program_id(1)

    @pl.when(c == 0)
    def _():
        s_ref[...] = s0_ref[...]

    glogit = jnp.dot(g_ref[...].astype(BF16), wgk_ref[...], preferred_element_type=F32) + bgk_ref[...]
    la = _log_sigmoid(glogit) * (1.0 / GLA_GATE_NORM)
    bc = _dot_exact(_order_mask(L, rev).astype(F32), la)
    mask_h = _order_mask(H, rev)
    first = slice(H, L) if rev else slice(0, H)
    second = slice(0, H) if rev else slice(H, L)
    first_end = H if rev else H - 1
    end = 0 if rev else L - 1
    sel_end = (lax.broadcasted_iota(jnp.int32, (L, 128), 0) == end).astype(F32)
    scale = DQK ** -0.5

    def scores(qx, kx, bq, bk, anchor):
        qt = (qx * jnp.exp(bq - anchor)).astype(BF16)
        kt = (kx * jnp.exp(anchor - bk)).astype(BF16)
        return _dot_nt(qt, kt)

    for h in range(HEADS):
        hs = slice(h * DQK, (h + 1) * DQK)
        vs = slice(h * DV, (h + 1) * DV)
        bch = bc[:, hs]
        qh = q_ref[:, hs] * scale
        kh = k_ref[:, hs]
        vh = v_ref[:, vs].astype(BF16)
        s_old = s_ref[h]
        o = jnp.dot((qh * jnp.exp(bch)).astype(BF16), s_old.astype(BF16), preferred_element_type=F32)

        def diag(rows):
            mid = rows.start + H // 2
            a = scores(qh[rows], kh[rows], bch[rows], bch[rows], bch[mid:mid + 1, :])
            return jnp.where(mask_h, a, 0.0).astype(BF16)

        a_ff = diag(first)
        a_ss = diag(second)
        a_sf = scores(qh[second], kh[first], bch[second], bch[first],
                      bch[first_end:first_end + 1, :]).astype(BF16)
        o_ref[first, vs] = o[first] + jnp.dot(a_ff, vh[first], preferred_element_type=F32)
        o_ref[second, vs] = (o[second] + jnp.dot(a_sf, vh[first], preferred_element_type=F32)
                             + jnp.dot(a_ss, vh[second], preferred_element_type=F32))

        b_end = bch[end:end + 1, :]
        kdec = (kh * jnp.exp(b_end - bch)).astype(BF16)
        carry = jnp.exp(_dot_exact(bch.T, sel_end))
        s_ref[h] = _wide(carry) * s_old + _dot_tn(kdec, vh)

    if emit_state:
        @pl.when(c == nc - 1)
        def _():
            sf_ref[...] = s_ref[...]


def _gla_scan(proj, wgk, bgk, s0, *, batch, seq, direction, emit_state):
    L = SCAN_L
    nc = seq // L
    rev = direction == 1
    m = batch * seq

    def row(b, c):
        return b * nc + (nc - 1 - c if rev else c)

    in_specs = [
        pl.BlockSpec((L, HEADS * DQK), lambda b, c: (row(b, c), COL_BQ // (HEADS * DQK))),
        pl.BlockSpec((L, HEADS * DQK), lambda b, c: (row(b, c), COL_BK // (HEADS * DQK))),
        pl.BlockSpec((L, HEADS * DV), lambda b, c: (row(b, c), COL_BV // (HEADS * DV))),
        pl.BlockSpec((L, 128), lambda b, c: (row(b, c), COL_SMALL // 128)),
        pl.BlockSpec((None, 128, HEADS * DQK), lambda b, c: (direction, 0, 0)),
        pl.BlockSpec((None, 1, HEADS * DQK), lambda b, c: (direction, 0, 0)),
        pl.BlockSpec((None, None, HEADS, DQK, DV), lambda b, c: (b, direction, 0, 0, 0)),
    ]
    out_specs = [pl.BlockSpec((L, HEADS * DV), lambda b, c: (row(b, c), 0))]
    out_shape = [jax.ShapeDtypeStruct((m, HEADS * DV), F32)]
    if emit_state:
        out_specs.append(pl.BlockSpec((None, HEADS, DQK, DV), lambda b, c: (b, 0, 0, 0)))
        out_shape.append(jax.ShapeDtypeStruct((batch, HEADS, DQK, DV), F32))
    return pl.pallas_call(
        functools.partial(_gla_kernel, nc=nc, rev=rev, emit_state=emit_state),
        grid=(batch, nc),
        in_specs=in_specs,
        out_specs=out_specs,
        out_shape=out_shape,
        scratch_shapes=[pltpu.VMEM((HEADS, DQK, DV), F32)],
        compiler_params=_cparams(("parallel", "arbitrary")),
        name="gla_scan",
    )(proj, proj, proj, proj, wgk, bgk, s0)


def _head_rms(x):
    parts = []
    for h in range(HEADS):
        xh = x[:, h * DV:(h + 1) * DV]
        ms = jnp.mean(xh * xh, axis=-1, keepdims=True)
        parts.append(xh * lax.rsqrt(ms + RMS_EPS))
    return jnp.concatenate(parts, axis=1)


def _mixout_kernel(x_ref, mod_ref, hmf_ref, hmr_ref, hgf_ref, hgr_ref, ao_ref, bg_ref,
                   ag_ref, bgn_ref, w_ref, o_ref):
    ya = _head_rms(hmf_ref[...] + hmr_ref[...]) * ag_ref[...] * jax.nn.sigmoid(ao_ref[...])
    bg = bg_ref[...]
    yb = _head_rms(hgf_ref[...] + hgr_ref[...]) * bgn_ref[...] * (bg * jax.nn.sigmoid(bg))
    hw = HEADS * DV
    y = jnp.dot(ya.astype(BF16), w_ref[0:hw, :], preferred_element_type=F32)
    y = y + jnp.dot(yb.astype(BF16), w_ref[hw:2 * hw, :], preferred_element_type=F32)
    o_ref[...] = x_ref[...] + mod_ref[...] * y


def _mixer_out(x, gate, hmf, hmr, hgf, hgr, proj, a_gain, b_gain, w_out):
    m, d = x.shape
    tm = OUT_TM
    hw = HEADS * DV
    wide = lambda col: pl.BlockSpec((tm, hw), lambda i: (i, col))
    return pl.pallas_call(
        _mixout_kernel,
        grid=(m // tm,),
        in_specs=[
            pl.BlockSpec((tm, d), lambda i: (i, 0)),
            pl.BlockSpec((None, 1, d), _mod_index(tm)),
            wide(0), wide(0), wide(0), wide(0),
            wide(COL_AO // hw), wide(COL_BG // hw),
            pl.BlockSpec((1, hw), lambda i: (0, 0)),
            pl.BlockSpec((1, hw), lambda i: (0, 0)),
            pl.BlockSpec((2 * hw, d), lambda i: (0, 0)),
        ],
        out_specs=pl.BlockSpec((tm, d), lambda i: (i, 0)),
        out_shape=jax.ShapeDtypeStruct((m, d), F32),
        compiler_params=_cparams(("parallel",)),
        name="mixer_out",
    )(x, gate, hmf, hmr, hgf, hgr, proj, proj, a_gain, b_gain, w_out)


def _rope(x, cos, sin):
    n = x.shape[1]
    lane = lax.broadcasted_iota(jnp.int32, x.shape, 1)
    partner = jnp.where((lane % 64) < 32, pltpu.roll(x, n - 32, 1), pltpu.roll(x, 32, 1))
    reps = n // HEAD_DIM
    cos_t = jnp.concatenate([cos] * reps, axis=1) if reps > 1 else cos
    sin_t = jnp.concatenate([sin] * reps, axis=1) if reps > 1 else sin
    return x * cos_t + partner * sin_t


def _head_norm(x, g, n_heads):
    parts = []
    for h in range(n_heads):
        xh = x[:, h * HEAD_DIM:(h + 1) * HEAD_DIM]
        ms = jnp.mean(xh * xh, axis=-1, keepdims=True)
        parts.append(xh * lax.rsqrt(ms + RMS_EPS) * g)
    return jnp.concatenate(parts, axis=1)


def _qkprep_kernel(q_in, k_in, v_in, qg_ref, kg_ref, *rest, rope):
    if rope:
        cos_ref, sin_ref, q_ref, k_ref, v_ref = rest
    else:
        q_ref, k_ref, v_ref, kf_ref = rest
    q = _head_norm(q_in[...], qg_ref[...], N_Q_HEADS)
    k = _head_norm(k_in[...], kg_ref[...], HEADS)
    if rope:
        q = _rope(q, cos_ref[...], sin_ref[...])
        k = _rope(k, cos_ref[...], sin_ref[...])
    else:
        kf_ref[...] = k
    q_ref[...] = q.astype(BF16)
    k_ref[...] = k.astype(BF16)
    v_ref[...] = v_in[...].astype(BF16)


def _qk_prep(proj, qg, kg, rope_tabs):
    m = proj.shape[0]
    tr = PREP_TR
    nq = N_Q_HEADS * HEAD_DIM
    nk = HEADS * HEAD_DIM
    rope = rope_tabs is not None
    in_specs = [
        pl.BlockSpec((tr, nq), lambda i: (i, 0)),
        pl.BlockSpec((tr, nk), lambda i: (i, nq // nk)),
        pl.BlockSpec((tr, nk), lambda i: (i, nq // nk + 1)),
        pl.BlockSpec((1, HEAD_DIM), lambda i: (0, 0)),
        pl.BlockSpec((1, HEAD_DIM), lambda i: (0, 0)),
    ]
    args = [proj, proj, proj, qg, kg]
    out_specs = [
        pl.BlockSpec((tr, nq), lambda i: (i, 0)),
        pl.BlockSpec((tr, nk), lambda i: (i, 0)),
        pl.BlockSpec((tr, nk), lambda i: (i, 0)),
    ]
    out_shape = [
        jax.ShapeDtypeStruct((m, nq), BF16),
        jax.ShapeDtypeStruct((m, nk), BF16),
        jax.ShapeDtypeStruct((m, nk), BF16),
    ]
    if rope:
        nt = rope_tabs[0].shape[0] // tr
        in_specs += [pl.BlockSpec((tr, HEAD_DIM), lambda i: (i % nt, 0))] * 2
        args += list(rope_tabs)
    else:
        out_specs.append(pl.BlockSpec((tr, nk), lambda i: (i, 0)))
        out_shape.append(jax.ShapeDtypeStruct((m, nk), F32))
    return pl.pallas_call(
        functools.partial(_qkprep_kernel, rope=rope),
        grid=(m // tr,),
        in_specs=in_specs,
        out_specs=out_specs,
        out_shape=out_shape,
        compiler_params=_cparams(("parallel",)),
        name="qk_prep",
    )(*args)


def _attn_kernel(q_ref, k_ref, v_ref, o_ref, *, tq):
    q4 = jnp.concatenate([q_ref[:, j * HEAD_DIM:(j + 1) * HEAD_DIM] for j in range(Q_PER_KV)], axis=0)
    sc = _dot_nt(q4, k_ref[...]) * (HEAD_DIM ** -0.5)
    p = jnp.exp(sc - jnp.max(sc, axis=-1, keepdims=True))
    p = p * (1.0 / jnp.sum(p, axis=-1, keepdims=True))
    o = jnp.dot(p.astype(BF16), v_ref[...], preferred_element_type=F32)
    for j in range(Q_PER_KV):
        o_ref[:, j * HEAD_DIM:(j + 1) * HEAD_DIM] = o[j * tq:(j + 1) * tq].astype(BF16)


def _attention(q, k, v, *, batch, sq, sk):
    tq = ATT_TQ
    nq = sq // tq
    gw = Q_PER_KV * HEAD_DIM
    return pl.pallas_call(
        functools.partial(_attn_kernel, tq=tq),
        grid=(batch, HEADS, nq),
        in_specs=[
            pl.BlockSpec((tq, gw), lambda b, g, i: (b * nq + i, g)),
            pl.BlockSpec((sk, HEAD_DIM), lambda b, g, i: (b, g)),
            pl.BlockSpec((sk, HEAD_DIM), lambda b, g, i: (b, g)),
        ],
        out_specs=pl.BlockSpec((tq, gw), lambda b, g, i: (b * nq + i, g)),
        out_shape=jax.ShapeDtypeStruct((batch * sq, N_Q_HEADS * HEAD_DIM), BF16),
        compiler_params=_cparams(("parallel", "parallel", "arbitrary")),
        name="gqa_attention",
    )(q, k, v)


def _oproj_kernel(x_ref, mod_ref, a_ref, w_ref, o_ref):
    y = jnp.dot(a_ref[...], w_ref[...], preferred_element_type=F32)
    o_ref[...] = x_ref[...] + mod_ref[...] * y


def _attn_out(x, gate, a, w_o):
    m, d = x.shape
    tm = PROJ_TM
    return pl.pallas_call(
        _oproj_kernel,
        grid=(m // tm,),
        in_specs=[
            pl.BlockSpec((tm, d), lambda i: (i, 0)),
            pl.BlockSpec((None, 1, d), _mod_index(tm)),
            pl.BlockSpec((tm, a.shape[1]), lambda i: (i, 0)),
            pl.BlockSpec(w_o.shape, lambda i: (0, 0)),
        ],
        out_specs=pl.BlockSpec((tm, d), lambda i: (i, 0)),
        out_shape=jax.ShapeDtypeStruct((m, d), F32),
        compiler_params=_cparams(("parallel",)),
        name="attn_out",
    )(x, gate, a, w_o)


def _permute_w_in(w):
    a_g = w[:, 3072:3088]
    main = jnp.concatenate([w[:, :3072], w[:, 3088:6160]], axis=1)
    small = jnp.concatenate([a_g, w[:, 6160:6192]], axis=1)
    pad = jnp.zeros((w.shape[0], AB_PAD_COLS - 6192), w.dtype)
    return jnp.concatenate([main, small, pad], axis=1).astype(BF16)


def _rope_tables(n_tokens):
    t = jnp.arange(n_tokens)
    row = (t // GRID_W).astype(F32)
    col = (t % GRID_W).astype(F32)
    inv = ROPE_THETA ** (-jnp.arange(0, ROPE_AXIS, 2, dtype=F32) / ROPE_AXIS)
    ar = row[:, None] * inv
    ac = col[:, None] * inv
    cos = jnp.concatenate([jnp.cos(ar), jnp.cos(ar), jnp.cos(ac), jnp.cos(ac)], axis=1)
    sin = jnp.concatenate([-jnp.sin(ar), jnp.sin(ar), -jnp.sin(ac), jnp.sin(ac)], axis=1)
    return cos, sin


def kernel(x_prompt, x_sample, c, c_ctx, state_mlstm_C, state_mlstm_n, state_mlstm_m, state_gla_S,
           cache_k, cache_v, w_mod, b_mod, norm_g, ffn_w_gate, ffn_w_up, ffn_w_down, w_in_ab,
           mlstm_conv_w, mlstm_conv_b, mlstm_b_i, mlstm_b_f, mlstm_out_g, gla_w_gk, gla_b_gk, gla_out_g,
           w_out_ab, w_qkv, q_norm_g, k_norm_g, w_o):
    bp, sp, d = x_prompt.shape
    bs, ss, _ = x_sample.shape
    depth = w_mod.shape[0]
    xp = x_prompt.reshape(bp * sp, d)
    xs = x_sample.reshape(bs * ss, d)

    c8 = jnp.concatenate([c_ctx[None, :], c, jnp.zeros((8 - 1 - bs, d), F32)], axis=0)
    mod = _modulation(c8, w_mod, b_mod).reshape(depth, 8, N_MOD, d)

    new_c, new_n, new_m, new_s, new_k, new_v = [], [], [], [], [], []
    for l in range(depth):
        mod_p = mod[l, 0:1]
        mod_s = mod[l, 1:1 + bs]

        ffn_w = [(ffn_w_gate[l, half].astype(BF16), ffn_w_up[l, half].astype(BF16),
                  ffn_w_down[l, half].astype(BF16)) for half in range(2)]

        def ffn_half(x, md, j, half):
            return _ffn(x, md[:, 3 * j:3 * j + 3], norm_g[l, j][None, :], *ffn_w[half])

        xp = ffn_half(xp, mod_p, 0, 0)
        xs = ffn_half(xs, mod_s, 0, 0)
        g_mix = norm_g[l, 1][None, :]
        if l % 2 == 0:
            e = l // 2
            w_in = _permute_w_in(w_in_ab[e])
            w_out = w_out_ab[e].astype(BF16)
            gbias = jnp.zeros((2, 2, HEADS), F32)
            gbias = gbias.at[:, 0].set(mlstm_b_i[e]).at[:, 1].set(mlstm_b_f[e])
            gbias = jnp.pad(gbias.reshape(1, 16), ((0, 0), (0, 128 - 16)))
            wgk = jnp.zeros((2, 128, HEADS * DQK), F32)
            for j in range(2):
                lo = LR_OFF + j * GLA_RANK
                wgk = wgk.at[j, lo:lo + GLA_RANK].set(gla_w_gk[e, j])
            wgk = wgk.astype(BF16)
            bgk = gla_b_gk[e][:, None, :]
            conv_w = mlstm_conv_w[e]
            conv_b = mlstm_conv_b[e][None, :]
            a_gain = mlstm_out_g[e][None, :]
            b_gain = gla_out_g[e][None, :]

            def mixer(x, md, batch, seq, c0, n0, m0, s0, emit_state):
                proj = _proj(x, md[:, 3:6], g_mix, w_in, 1280)
                q, k = _conv_qk(proj, conv_w, conv_b, seq)
                m0r = jnp.broadcast_to(m0[..., None], m0.shape + (128,))
                hm, hg, states = [], [], []
                for direction in range(2):
                    r = _mlstm_scan(q, k, proj, gbias, c0, n0, m0r, batch=batch, seq=seq,
                                    direction=direction, emit_state=emit_state)
                    hm.append(r[0])
                    r2 = _gla_scan(proj, wgk, bgk, s0, batch=batch, seq=seq,
                                   direction=direction, emit_state=emit_state)
                    hg.append(r2[0])
                    if emit_state:
                        states.append((r[1], r[2], r[3][..., 0], r2[1]))
                y = _mixer_out(x, md[:, 5:6], hm[0], hm[1], hg[0], hg[1], proj, a_gain, b_gain, w_out)
                return y, states

            zc = jnp.zeros((bp, 2, HEADS, DQK, DV), F32)
            zn = jnp.zeros((bp, 2, HEADS, DQK), F32)
            zm = jnp.zeros((bp, 2, HEADS), F32)
            xp, st = mixer(xp, mod_p, bp, sp, zc, zn, zm, zc, True)
            xs, _ = mixer(xs, mod_s, bs, ss, state_mlstm_C[:, e], state_mlstm_n[:, e],
                          state_mlstm_m[:, e], state_gla_S[:, e], False)
            for lst, idx in ((new_c, 0), (new_n, 1), (new_m, 2), (new_s, 3)):
                lst.append(jnp.stack([st[0][idx], st[1][idx]], axis=1))
        else:
            o = l // 2
            wq = w_qkv[o].astype(BF16)
            wo = w_o[o].astype(BF16)
            qg = q_norm_g[o][None, :]
            kg = k_norm_g[o][None, :]
            nk = HEADS * HEAD_DIM
            proj_p = _proj(xp, mod_p[:, 3:6], g_mix, wq, 1024)
            q, k, v, kf = _qk_prep(proj_p, qg, kg, None)
            a = _attention(q, k, v, batch=bp, sq=sp, sk=sp)
            xp = _attn_out(xp, mod_p[:, 5:6], a, wo)
            new_k.append(kf.reshape(bp, sp, HEADS, HEAD_DIM))
            new_v.append(proj_p[:, N_Q_HEADS * HEAD_DIM + nk:].reshape(bp, sp, HEADS, HEAD_DIM))
            proj_s = _proj(xs, mod_s[:, 3:6], g_mix, wq, 1024)
            q, k, v = _qk_prep(proj_s, qg, kg, _rope_tables(ss))
            past = cache_k.shape[2]
            ck = cache_k[:, o].reshape(bs, past, nk).astype(BF16)
            cv = cache_v[:, o].reshape(bs, past, nk).astype(BF16)
            k_all = jnp.concatenate([ck, k.reshape(bs, ss, nk)], axis=1).reshape(bs * (past + ss), nk)
            v_all = jnp.concatenate([cv, v.reshape(bs, ss, nk)], axis=1).reshape(bs * (past + ss), nk)
            a = _attention(q, k_all, v_all, batch=bs, sq=ss, sk=past + ss)
            xs = _attn_out(xs, mod_s[:, 5:6], a, wo)
        xp = ffn_half(xp, mod_p, 2, 1)
        xs = ffn_half(xs, mod_s, 2, 1)

    return (xp.reshape(bp, sp, d), xs.reshape(bs, ss, d),
            jnp.stack(new_c, axis=1), jnp.stack(new_n, axis=1), jnp.stack(new_m, axis=1),
            jnp.stack(new_s, axis=1), jnp.stack(new_k, axis=1), jnp.stack(new_v, axis=1))
```

```python
import functools

import jax
import jax.numpy as jnp
from jax import lax
from jax.experimental import pallas as pl
from jax.experimental.pallas import tpu as pltpu

F32 = jnp.float32
BF16 = jnp.bfloat16
HIGHEST = lax.Precision.HIGHEST

D_MODEL = 2048
N_MOD = 9
D_FF = 5632
RMS_EPS = 1e-6
ROWS_PER_MOD = 4096
HEADS = 4
DQK = 128
DV = 256
GLA_RANK = 16
GLA_GATE_NORM = 16.0
CONV_W = 3
N_Q_HEADS = 16
Q_PER_KV = 4
HEAD_DIM = 128
GRID_W = 64
ROPE_AXIS = HEAD_DIM // 2
ROPE_THETA = 10000.0

COL_AQ, COL_AK, COL_AV, COL_AO = 0, 512, 1024, 2048
COL_BQ, COL_BK, COL_BV, COL_BG = 3072, 3584, 4096, 5120
COL_SMALL = 6144
AB_PAD_COLS = 6400
LR_OFF = 16

VMEM_LIMIT = 60 * 1024 * 1024
SCAN_L = 128
FFN_TM, FFN_TF = 1024, 512
PROJ_TM = 1024
ATTN_OUT_TM = 512
OUT_TM = 256
PREP_TR = 256
ATT_TQ = 256
ATT_KC = 256
LOG2E = 1.4426950408889634


def _cparams(sem):
    return pltpu.CompilerParams(dimension_semantics=sem, vmem_limit_bytes=VMEM_LIMIT)


def _log_sigmoid(x):
    return jnp.minimum(x, 0.0) - jnp.log1p(jnp.exp(-jnp.abs(x)))


def _adaln(x, g, shift, scale):
    ms = jnp.mean(x * x, axis=-1, keepdims=True)
    return (x * lax.rsqrt(ms + RMS_EPS) * g) * (1.0 + scale) + shift


def _mod_index(tm):
    return lambda i, *_: ((i * tm) // ROWS_PER_MOD, 0, 0)


def _mod_kernel(c_ref, w_ref, b_ref, o_ref):
    c = c_ref[...]
    s = (c * jax.nn.sigmoid(c)).astype(BF16)
    o_ref[...] = jnp.dot(s, w_ref[...].astype(BF16), preferred_element_type=F32) + b_ref[...]


def _modulation(c8, w_mod, b_mod):
    depth, d, n = w_mod.shape
    tn = 1024
    return pl.pallas_call(
        _mod_kernel,
        grid=(depth, n // tn),
        in_specs=[
            pl.BlockSpec((8, d), lambda l, j: (0, 0)),
            pl.BlockSpec((None, d, tn), lambda l, j: (l, 0, j)),
            pl.BlockSpec((None, 1, tn), lambda l, j: (l, 0, j)),
        ],
        out_specs=pl.BlockSpec((None, 8, tn), lambda l, j: (l, 0, j)),
        out_shape=jax.ShapeDtypeStruct((depth, 8, n), F32),
        compiler_params=_cparams(("arbitrary", "arbitrary")),
        name="modulation",
    )(c8, w_mod, b_mod.reshape(depth, 1, n))


def _ffn_kernel(x_ref, mod_ref, g_ref, wg_ref, wu_ref, wd_ref, o_ref, h_ref, *, nf):
    f = pl.program_id(1)

    def swiglu_part(h):
        gate = jnp.dot(h, wg_ref[...], preferred_element_type=F32)
        up = jnp.dot(h, wu_ref[...], preferred_element_type=F32)
        a = (gate * jax.nn.sigmoid(gate) * up).astype(BF16)
        return jnp.dot(a, wd_ref[...], preferred_element_type=F32)

    @pl.when(f == 0)
    def _():
        h = _adaln(x_ref[...], g_ref[...], mod_ref[0:1, :], mod_ref[1:2, :]).astype(BF16)
        h_ref[...] = h
        o_ref[...] = swiglu_part(h)

    @pl.when(jnp.logical_and(f > 0, f < nf - 1))
    def _():
        o_ref[...] += swiglu_part(h_ref[...])

    @pl.when(f == nf - 1)
    def _():
        acc = o_ref[...] + swiglu_part(h_ref[...])
        o_ref[...] = x_ref[...] + (0.5 * mod_ref[2:3, :]) * acc


def _ffn(x, mod3, g, wg, wu, wd):
    m, d = x.shape
    tm, tf = FFN_TM, FFN_TF
    nf = D_FF // tf
    return pl.pallas_call(
        functools.partial(_ffn_kernel, nf=nf),
        grid=(m // tm, nf),
        in_specs=[
            pl.BlockSpec((tm, d), lambda i, f: (i, 0)),
            pl.BlockSpec((None, 3, d), _mod_index(tm)),
            pl.BlockSpec((1, d), lambda i, f: (0, 0)),
            pl.BlockSpec((d, tf), lambda i, f: (0, f)),
            pl.BlockSpec((d, tf), lambda i, f: (0, f)),
            pl.BlockSpec((tf, d), lambda i, f: (f, 0)),
        ],
        out_specs=pl.BlockSpec((tm, d), lambda i, f: (i, 0)),
        out_shape=jax.ShapeDtypeStruct((m, d), F32),
        scratch_shapes=[pltpu.VMEM((tm, d), BF16)],
        compiler_params=_cparams(("parallel", "arbitrary")),
        name="ffn",
    )(x, mod3, g, wg, wu, wd)


def _proj_kernel(x_ref, mod_ref, g_ref, w_ref, o_ref, h_ref):
    j = pl.program_id(1)

    @pl.when(j == 0)
    def _():
        h = _adaln(x_ref[...], g_ref[...], mod_ref[0:1, :], mod_ref[1:2, :]).astype(BF16)
        h_ref[...] = h
        o_ref[...] = jnp.dot(h, w_ref[...], preferred_element_type=F32)

    @pl.when(j > 0)
    def _():
        o_ref[...] = jnp.dot(h_ref[...], w_ref[...], preferred_element_type=F32)


def _proj(x, mod3, g, w, tn):
    m, d = x.shape
    n = w.shape[1]
    tm = PROJ_TM
    return pl.pallas_call(
        _proj_kernel,
        grid=(m // tm, n // tn),
        in_specs=[
            pl.BlockSpec((tm, d), lambda i, j: (i, 0)),
            pl.BlockSpec((None, 3, d), _mod_index(tm)),
            pl.BlockSpec((1, d), lambda i, j: (0, 0)),
            pl.BlockSpec((d, tn), lambda i, j: (0, j)),
        ],
        out_specs=pl.BlockSpec((tm, tn), lambda i, j: (i, j)),
        out_shape=jax.ShapeDtypeStruct((m, n), F32),
        scratch_shapes=[pltpu.VMEM((tm, d), BF16)],
        compiler_params=_cparams(("parallel", "arbitrary")),
        name="adaln_proj",
    )(x, mod3, g, w)


def _conv_kernel(x_ref, prev_ref, next_ref, w_ref, b_ref, q_ref, k_ref, *, tr, seq):
    i = pl.program_id(0)
    x = x_ref[...]
    first = (i * tr) % seq == 0
    last = ((i + 1) * tr) % seq == 0
    prev_row = jnp.where(first, 0.0, prev_ref[7:8, :])
    next_row = jnp.where(last, 0.0, next_ref[0:1, :])
    ridx = lax.broadcasted_iota(jnp.int32, x.shape, 0)
    xm = jnp.where(ridx == 0, prev_row, pltpu.roll(x, 1, 0))
    xp = jnp.where(ridx == tr - 1, next_row, pltpu.roll(x, tr - 1, 0))
    y = xm * w_ref[0:1, :] + x * w_ref[1:2, :] + xp * w_ref[2:3, :] + b_ref[...]
    y = y * jax.nn.sigmoid(y)
    hqk = HEADS * DQK
    q_ref[...] = y[:, :hqk].astype(BF16)
    k_ref[...] = (y[:, hqk:] * (DQK ** -0.5)).astype(BF16)


def _conv_qk(proj, conv_w, conv_b, seq):
    m = proj.shape[0]
    tr = PREP_TR
    c = 2 * HEADS * DQK
    nb8 = m // 8
    return pl.pallas_call(
        functools.partial(_conv_kernel, tr=tr, seq=seq),
        grid=(m // tr,),
        in_specs=[
            pl.BlockSpec((tr, c), lambda i: (i, 0)),
            pl.BlockSpec((8, c), lambda i: (jnp.maximum(i * (tr // 8) - 1, 0), 0)),
            pl.BlockSpec((8, c), lambda i: (jnp.minimum((i + 1) * (tr // 8), nb8 - 1), 0)),
            pl.BlockSpec((CONV_W, c), lambda i: (0, 0)),
            pl.BlockSpec((1, c), lambda i: (0, 0)),
        ],
        out_specs=[
            pl.BlockSpec((tr, c // 2), lambda i: (i, 0)),
            pl.BlockSpec((tr, c // 2), lambda i: (i, 0)),
        ],
        out_shape=[jax.ShapeDtypeStruct((m, c // 2), BF16)] * 2,
        compiler_params=_cparams(("parallel",)),
        name="mlstm_conv",
    )(proj, proj, proj, conv_w, conv_b)


def _order_mask(n, rev):
    t = lax.broadcasted_iota(jnp.int32, (n, n), 0)
    s = lax.broadcasted_iota(jnp.int32, (n, n), 1)
    return (s >= t) if rev else (s <= t)


def _lane_select(cols):
    r = lax.broadcasted_iota(jnp.int32, (128, 128), 0)
    return jnp.concatenate([(r == c).astype(F32) for c in cols], axis=1)


def _dot_exact(a, b):
    return jnp.dot(a, b, precision=HIGHEST, preferred_element_type=F32)


def _dot_nt(a, b):
    return lax.dot_general(a, b, (((1,), (1,)), ((), ())), preferred_element_type=F32)


def _dot_tn(a, b):
    return lax.dot_general(a, b, (((0,), (0,)), ((), ())), preferred_element_type=F32)


def _wide(x):
    return jnp.concatenate([x, x], axis=1)


def _mlstm_kernel(q_ref, k_ref, v_ref, g_ref, gb_ref, c0_ref, n0_ref, m0_ref, *rest,
                  nc, rev, direction, emit_state):
    if emit_state:
        h_ref, cf_ref, nf_ref, mf_ref, c_ref, n_ref, m_ref = rest
    else:
        h_ref, c_ref, n_ref, m_ref = rest
    L = SCAN_L
    c = pl.program_id(1)

    @pl.when(c == 0)
    def _():
        c_ref[...] = c0_ref[...]
        n_ref[...] = n0_ref[...]
        m_ref[...] = m0_ref[...]

    mask = _order_mask(L, rev)
    gates = g_ref[...] + gb_ref[...]
    logf = _log_sigmoid(gates)
    bsum = _dot_exact(mask.astype(F32), logf)
    col_i = [8 * direction + h for h in range(HEADS)]
    col_f = [8 * direction + HEADS + h for h in range(HEADS)]
    i_rep = _dot_exact(gates, _lane_select(col_i))
    b_rep = _dot_exact(bsum, _lane_select(col_f))
    gates_t = gates.T
    bsum_t = bsum.T
    end = 0 if rev else L - 1

    for h in range(HEADS):
        hs = slice(h * DQK, (h + 1) * DQK)
        vs = slice(h * DV, (h + 1) * DV)
        b_c = b_rep[:, hs]
        i_c = i_rep[:, hs]
        b_r = bsum_t[col_f[h]:col_f[h] + 1, :]
        i_r = gates_t[col_i[h]:col_i[h] + 1, :]
        m_prev = m_ref[h:h + 1, :]
        dmat = jnp.where(mask, b_c - b_r + i_r, -jnp.inf)
        inter = b_c + m_prev
        m_t = jnp.maximum(inter, jnp.max(dmat, axis=-1, keepdims=True))
        w_intra = jnp.exp(dmat - m_t)
        w_inter = jnp.exp(inter - m_t)
        qh = q_ref[:, hs]
        kh = k_ref[:, hs]
        vh = v_ref[:, vs].astype(BF16)
        s = _dot_nt(qh, kh) * w_intra
        c_old = c_ref[h]
        num = _wide(w_inter) * jnp.dot(qh, c_old.astype(BF16), preferred_element_type=F32)
        num = num + jnp.dot(s.astype(BF16), vh, preferred_element_type=F32)
        qn = jnp.sum(qh.astype(F32) * n_ref[h:h + 1, :], axis=-1, keepdims=True)
        den = w_inter * qn + jnp.sum(s, axis=-1, keepdims=True)
        inv = 1.0 / jnp.maximum(jnp.abs(den), jnp.exp(-m_t))
        h_ref[:, vs] = num * _wide(inv)

        g_end = b_c[end:end + 1, :]
        dec = g_end - b_c + i_c
        m_new = jnp.maximum(g_end + m_prev, jnp.max(dec, axis=0, keepdims=True))
        ws = jnp.exp(dec - m_new)
        wc = jnp.exp(g_end + m_prev - m_new)
        kw = kh.astype(F32) * ws
        c_ref[h] = _wide(wc) * c_old + _dot_tn(kw.astype(BF16), vh)
        n_ref[h:h + 1, :] = wc * n_ref[h:h + 1, :] + jnp.sum(kw, axis=0, keepdims=True)
        m_ref[h:h + 1, :] = m_new

    if emit_state:
        @pl.when(c == nc - 1)
        def _():
            cf_ref[...] = c_ref[...]
            nf_ref[...] = n_ref[...]
            mf_ref[...] = m_ref[...]


def _mlstm_scan(q, k, proj, gbias, c0, n0, m0, *, batch, seq, direction, emit_state):
    L = SCAN_L
    nc = seq // L
    rev = direction == 1
    m = batch * seq

    def row(b, c):
        return b * nc + (nc - 1 - c if rev else c)

    in_specs = [
        pl.BlockSpec((L, HEADS * DQK), lambda b, c: (row(b, c), 0)),
        pl.BlockSpec((L, HEADS * DQK), lambda b, c: (row(b, c), 0)),
        pl.BlockSpec((L, HEADS * DV), lambda b, c: (row(b, c), COL_AV // (HEADS * DV))),
        pl.BlockSpec((L, 128), lambda b, c: (row(b, c), COL_SMALL // 128)),
        pl.BlockSpec((1, 128), lambda b, c: (0, 0)),
        pl.BlockSpec((None, None, HEADS, DQK, DV), lambda b, c: (b, direction, 0, 0, 0)),
        pl.BlockSpec((None, None, HEADS, DQK), lambda b, c: (b, direction, 0, 0)),
        pl.BlockSpec((None, None, HEADS, 128), lambda b, c: (b, direction, 0, 0)),
    ]
    out_specs = [pl.BlockSpec((L, HEADS * DV), lambda b, c: (row(b, c), 0))]
    out_shape = [jax.ShapeDtypeStruct((m, HEADS * DV), F32)]
    if emit_state:
        out_specs += [
            pl.BlockSpec((None, HEADS, DQK, DV), lambda b, c: (b, 0, 0, 0)),
            pl.BlockSpec((None, HEADS, DQK), lambda b, c: (b, 0, 0)),
            pl.BlockSpec((None, HEADS, 128), lambda b, c: (b, 0, 0)),
        ]
        out_shape += [
            jax.ShapeDtypeStruct((batch, HEADS, DQK, DV), F32),
            jax.ShapeDtypeStruct((batch, HEADS, DQK), F32),
            jax.ShapeDtypeStruct((batch, HEADS, 128), F32),
        ]
    return pl.pallas_call(
        functools.partial(_mlstm_kernel, nc=nc, rev=rev, direction=direction, emit_state=emit_state),
        grid=(batch, nc),
        in_specs=in_specs,
        out_specs=out_specs,
        out_shape=out_shape,
        scratch_shapes=[
            pltpu.VMEM((HEADS, DQK, DV), F32),
            pltpu.VMEM((HEADS, DQK), F32),
            pltpu.VMEM((HEADS, 128), F32),
        ],
        compiler_params=_cparams(("parallel", "arbitrary")),
        name="mlstm_scan",
    )(q, k, proj, proj, gbias, c0, n0, m0)


def _gla_kernel(q_ref, k_ref, v_ref, g_ref, wgk_ref, bgk_ref, s0_ref, *rest, nc, rev, emit_state):
    if emit_state:
        o_ref, sf_ref, s_ref = rest
    else:
        o_ref, s_ref = rest
    L = SCAN_L
    H = L // 2
    c = pl.program_id(1)

    @pl.when(c == 0)
    def _():
        s_ref[...] = s0_ref[...]

    glogit = jnp.dot(g_ref[...].astype(BF16), wgk_ref[...], preferred_element_type=F32) + bgk_ref[...]
    la = _log_sigmoid(glogit) * (1.0 / GLA_GATE_NORM)
    bc = _dot_exact(_order_mask(L, rev).astype(F32), la)
    mask_h = _order_mask(H, rev)
    first = slice(H, L) if rev else slice(0, H)
    second = slice(0, H) if rev else slice(H, L)
    first_end = H if rev else H - 1
    end = 0 if rev else L - 1
    sel_end = (lax.broadcasted_iota(jnp.int32, (L, 128), 0) == end).astype(F32)
    scale = DQK ** -0.5

    def scores(qx, kx, bq, bk, anchor):
        qt = (qx * jnp.exp(bq - anchor)).astype(BF16)
        kt = (kx * jnp.exp(anchor - bk)).astype(BF16)
        return _dot_nt(qt, kt)

    for h in range(HEADS):
        hs = slice(h * DQK, (h + 1) * DQK)
        vs = slice(h * DV, (h + 1) * DV)
        bch = bc[:, hs]
        qh = q_ref[:, hs] * scale
        kh = k_ref[:, hs]
        vh = v_ref[:, vs].astype(BF16)
        s_old = s_ref[h]
        o = jnp.dot((qh * jnp.exp(bch)).astype(BF16), s_old.astype(BF16), preferred_element_type=F32)

        def diag(rows):
            mid = rows.start + H // 2
            a = scores(qh[rows], kh[rows], bch[rows], bch[rows], bch[mid:mid + 1, :])
            return jnp.where(mask_h, a, 0.0).astype(BF16)

        a_ff = diag(first)
        a_ss = diag(second)
        a_sf = scores(qh[second], kh[first], bch[second], bch[first],
                      bch[first_end:first_end + 1, :]).astype(BF16)
        o_ref[first, vs] = o[first] + jnp.dot(a_ff, vh[first], preferred_element_type=F32)
        o_ref[second, vs] = (o[second] + jnp.dot(a_sf, vh[first], preferred_element_type=F32)
                             + jnp.dot(a_ss, vh[second], preferred_element_type=F32))

        b_end = bch[end:end + 1, :]
        kdec = (kh * jnp.exp(b_end - bch)).astype(BF16)
        carry = jnp.exp(_dot_exact(bch.T, sel_end))
        s_ref[h] = _wide(carry) * s_old + _dot_tn(kdec, vh)

    if emit_state:
        @pl.when(c == nc - 1)
        def _():
            sf_ref[...] = s_ref[...]


def _gla_scan(proj, wgk, bgk, s0, *, batch, seq, direction, emit_state):
    L = SCAN_L
    nc = seq // L
    rev = direction == 1
    m = batch * seq

    def row(b, c):
        return b * nc + (nc - 1 - c if rev else c)

    in_specs = [
        pl.BlockSpec((L, HEADS * DQK), lambda b, c: (row(b, c), COL_BQ // (HEADS * DQK))),
        pl.BlockSpec((L, HEADS * DQK), lambda b, c: (row(b, c), COL_BK // (HEADS * DQK))),
        pl.BlockSpec((L, HEADS * DV), lambda b, c: (row(b, c), COL_BV // (HEADS * DV))),
        pl.BlockSpec((L, 128), lambda b, c: (row(b, c), COL_SMALL // 128)),
        pl.BlockSpec((None, 128, HEADS * DQK), lambda b, c: (direction, 0, 0)),
        pl.BlockSpec((None, 1, HEADS * DQK), lambda b, c: (direction, 0, 0)),
        pl.BlockSpec((None, None, HEADS, DQK, DV), lambda b, c: (b, direction, 0, 0, 0)),
    ]
    out_specs = [pl.BlockSpec((L, HEADS * DV), lambda b, c: (row(b, c), 0))]
    out_shape = [jax.ShapeDtypeStruct((m, HEADS * DV), F32)]
    if emit_state:
        out_specs.append(pl.BlockSpec((None, HEADS, DQK, DV), lambda b, c: (b, 0, 0, 0)))
        out_shape.append(jax.ShapeDtypeStruct((batch, HEADS, DQK, DV), F32))
    return pl.pallas_call(
        functools.partial(_gla_kernel, nc=nc, rev=rev, emit_state=emit_state),
        grid=(batch, nc),
        in_specs=in_specs,
        out_specs=out_specs,
        out_shape=out_shape,
        scratch_shapes=[pltpu.VMEM((HEADS, DQK, DV), F32)],
        compiler_params=_cparams(("parallel", "arbitrary")),
        name="gla_scan",
    )(proj, proj, proj, proj, wgk, bgk, s0)


def _head_rms(x):
    parts = []
    for h in range(HEADS):
        xh = x[:, h * DV:(h + 1) * DV]
        ms = jnp.mean(xh * xh, axis=-1, keepdims=True)
        parts.append(xh * lax.rsqrt(ms + RMS_EPS))
    return jnp.concatenate(parts, axis=1)


def _mixout_kernel(x_ref, mod_ref, hmf_ref, hmr_ref, hgf_ref, hgr_ref, ao_ref, bg_ref,
                   ag_ref, bgn_ref, w_ref, o_ref):
    ya = _head_rms(hmf_ref[...] + hmr_ref[...]) * ag_ref[...] * jax.nn.sigmoid(ao_ref[...])
    bg = bg_ref[...]
    yb = _head_rms(hgf_ref[...] + hgr_ref[...]) * bgn_ref[...] * (bg * jax.nn.sigmoid(bg))
    hw = HEADS * DV
    y = jnp.dot(ya.astype(BF16), w_ref[0:hw, :], preferred_element_type=F32)
    y = y + jnp.dot(yb.astype(BF16), w_ref[hw:2 * hw, :], preferred_element_type=F32)
    o_ref[...] = x_ref[...] + mod_ref[...] * y


def _mixer_out(x, gate, hmf, hmr, hgf, hgr, proj, a_gain, b_gain, w_out):
    m, d = x.shape
    tm = OUT_TM
    hw = HEADS * DV
    wide = lambda col: pl.BlockSpec((tm, hw), lambda i: (i, col))
    return pl.pallas_call(
        _mixout_kernel,
        grid=(m // tm,),
        in_specs=[
            pl.BlockSpec((tm, d), lambda i: (i, 0)),
            pl.BlockSpec((None, 1, d), _mod_index(tm)),
            wide(0), wide(0), wide(0), wide(0),
            wide(COL_AO // hw), wide(COL_BG // hw),
            pl.BlockSpec((1, hw), lambda i: (0, 0)),
            pl.BlockSpec((1, hw), lambda i: (0, 0)),
            pl.BlockSpec((2 * hw, d), lambda i: (0, 0)),
        ],
        out_specs=pl.BlockSpec((tm, d), lambda i: (i, 0)),
        out_shape=jax.ShapeDtypeStruct((m, d), F32),
        compiler_params=_cparams(("parallel",)),
        name="mixer_out",
    )(x, gate, hmf, hmr, hgf, hgr, proj, proj, a_gain, b_gain, w_out)


def _rope(x, cos, sin):
    n = x.shape[1]
    lane = lax.broadcasted_iota(jnp.int32, x.shape, 1)
    partner = jnp.where((lane % 64) < 32, pltpu.roll(x, n - 32, 1), pltpu.roll(x, 32, 1))
    reps = n // HEAD_DIM
    cos_t = jnp.concatenate([cos] * reps, axis=1) if reps > 1 else cos
    sin_t = jnp.concatenate([sin] * reps, axis=1) if reps > 1 else sin
    return x * cos_t + partner * sin_t


def _head_norm(x, g, n_heads):
    parts = []
    for h in range(n_heads):
        xh = x[:, h * HEAD_DIM:(h + 1) * HEAD_DIM]
        ms = jnp.mean(xh * xh, axis=-1, keepdims=True)
        parts.append(xh * lax.rsqrt(ms + RMS_EPS) * g)
    return jnp.concatenate(parts, axis=1)


def _qkprep_kernel(q_in, k_in, v_in, qg_ref, kg_ref, *rest, rope):
    if rope:
        cos_ref, sin_ref, q_ref, k_ref, v_ref = rest
    else:
        q_ref, k_ref, v_ref, kf_ref = rest
    q = _head_norm(q_in[...], qg_ref[...], N_Q_HEADS)
    k = _head_norm(k_in[...], kg_ref[...], HEADS)
    if rope:
        q = _rope(q, cos_ref[...], sin_ref[...])
        k = _rope(k, cos_ref[...], sin_ref[...])
    else:
        kf_ref[...] = k
    q_ref[...] = (q * (HEAD_DIM ** -0.5 * LOG2E)).astype(BF16)
    k_ref[...] = k.astype(BF16)
    v_ref[...] = v_in[...].astype(BF16)


def _qk_prep(proj, qg, kg, rope_tabs):
    m = proj.shape[0]
    tr = PREP_TR
    nq = N_Q_HEADS * HEAD_DIM
    nk = HEADS * HEAD_DIM
    rope = rope_tabs is not None
    in_specs = [
        pl.BlockSpec((tr, nq), lambda i: (i, 0)),
        pl.BlockSpec((tr, nk), lambda i: (i, nq // nk)),
        pl.BlockSpec((tr, nk), lambda i: (i, nq // nk + 1)),
        pl.BlockSpec((1, HEAD_DIM), lambda i: (0, 0)),
        pl.BlockSpec((1, HEAD_DIM), lambda i: (0, 0)),
    ]
    args = [proj, proj, proj, qg, kg]
    out_specs = [
        pl.BlockSpec((tr, nq), lambda i: (i, 0)),
        pl.BlockSpec((tr, nk), lambda i: (i, 0)),
        pl.BlockSpec((tr, nk), lambda i: (i, 0)),
    ]
    out_shape = [
        jax.ShapeDtypeStruct((m, nq), BF16),
        jax.ShapeDtypeStruct((m, nk), BF16),
        jax.ShapeDtypeStruct((m, nk), BF16),
    ]
    if rope:
        nt = rope_tabs[0].shape[0] // tr
        in_specs += [pl.BlockSpec((tr, HEAD_DIM), lambda i: (i % nt, 0))] * 2
        args += list(rope_tabs)
    else:
        out_specs.append(pl.BlockSpec((tr, nk), lambda i: (i, 0)))
        out_shape.append(jax.ShapeDtypeStruct((m, nk), F32))
    return pl.pallas_call(
        functools.partial(_qkprep_kernel, rope=rope),
        grid=(m // tr,),
        in_specs=in_specs,
        out_specs=out_specs,
        out_shape=out_shape,
        compiler_params=_cparams(("parallel",)),
        name="qk_prep",
    )(*args)


def _attn_kernel(q_ref, k_ref, v_ref, o_ref, sc_ref, *, tq, sk):
    kc = min(ATT_KC, sk)
    chunks = [slice(j * kc, (j + 1) * kc) for j in range(sk // kc)]
    for j in range(Q_PER_KV):
        hs = slice(j * HEAD_DIM, (j + 1) * HEAD_DIM)
        qj = q_ref[:, hs]
        mrun = None
        for ch in chunks:
            s = _dot_nt(qj, k_ref[ch, :])
            sc_ref[j, :, ch] = s
            for b in range(kc // 128):
                sb = s[:, b * 128:(b + 1) * 128]
                mrun = sb if mrun is None else jnp.maximum(mrun, sb)
        m = jnp.max(mrun, axis=-1, keepdims=True)
        lrun = jnp.zeros_like(mrun)
        acc = jnp.zeros((tq, HEAD_DIM), F32)
        for ch in chunks:
            p = jnp.exp2(sc_ref[j, :, ch] - m)
            for b in range(kc // 128):
                lrun = lrun + p[:, b * 128:(b + 1) * 128]
            acc = acc + jnp.dot(p.astype(BF16), v_ref[ch, :], preferred_element_type=F32)
        o = acc * (1.0 / jnp.sum(lrun, axis=-1, keepdims=True))
        o_ref[:, hs] = o.astype(BF16)


def _attention(q, k, v, *, batch, sq, sk):
    tq = ATT_TQ
    nq = sq // tq
    gw = Q_PER_KV * HEAD_DIM
    return pl.pallas_call(
        functools.partial(_attn_kernel, tq=tq, sk=sk),
        scratch_shapes=[pltpu.VMEM((Q_PER_KV, tq, sk), F32)],
        grid=(batch, HEADS, nq),
        in_specs=[
            pl.BlockSpec((tq, gw), lambda b, g, i: (b * nq + i, g)),
            pl.BlockSpec((sk, HEAD_DIM), lambda b, g, i: (b, g)),
            pl.BlockSpec((sk, HEAD_DIM), lambda b, g, i: (b, g)),
        ],
        out_specs=pl.BlockSpec((tq, gw), lambda b, g, i: (b * nq + i, g)),
        out_shape=jax.ShapeDtypeStruct((batch * sq, N_Q_HEADS * HEAD_DIM), BF16),
        compiler_params=_cparams(("parallel", "parallel", "arbitrary")),
        name="gqa_attention",
    )(q, k, v)


def _oproj_kernel(x_ref, mod_ref, a_ref, w_ref, o_ref):
    y = jnp.dot(a_ref[...], w_ref[...], preferred_element_type=F32)
    o_ref[...] = x_ref[...] + mod_ref[...] * y


def _attn_out(x, gate, a, w_o):
    m, d = x.shape
    tm = ATTN_OUT_TM
    return pl.pallas_call(
        _oproj_kernel,
        grid=(m // tm,),
        in_specs=[
            pl.BlockSpec((tm, d), lambda i: (i, 0)),
            pl.BlockSpec((None, 1, d), _mod_index(tm)),
            pl.BlockSpec((tm, a.shape[1]), lambda i: (i, 0)),
            pl.BlockSpec(w_o.shape, lambda i: (0, 0)),
        ],
        out_specs=pl.BlockSpec((tm, d), lambda i: (i, 0)),
        out_shape=jax.ShapeDtypeStruct((m, d), F32),
        compiler_params=_cparams(("parallel",)),
        name="attn_out",
    )(x, gate, a, w_o)


def _permute_w_in(w):
    a_g = w[:, 3072:3088]
    main = jnp.concatenate([w[:, :3072], w[:, 3088:6160]], axis=1)
    small = jnp.concatenate([a_g, w[:, 6160:6192]], axis=1)
    pad = jnp.zeros((w.shape[0], AB_PAD_COLS - 6192), w.dtype)
    return jnp.concatenate([main, small, pad], axis=1).astype(BF16)


def _rope_tables(n_tokens):
    t = jnp.arange(n_tokens)
    row = (t // GRID_W).astype(F32)
    col = (t % GRID_W).astype(F32)
    inv = ROPE_THETA ** (-jnp.arange(0, ROPE_AXIS, 2, dtype=F32) / ROPE_AXIS)
    ar = row[:, None] * inv
    ac = col[:, None] * inv
    cos = jnp.concatenate([jnp.cos(ar), jnp.cos(ar), jnp.cos(ac), jnp.cos(ac)], axis=1)
    sin = jnp.concatenate([-jnp.sin(ar), jnp.sin(ar), -jnp.sin(ac), jnp.sin(ac)], axis=1)
    return cos, sin


def kernel(x_prompt, x_sample, c, c_ctx, state_mlstm_C, state_mlstm_n, state_mlstm_m, state_gla_S,
           cache_k, cache_v, w_mod, b_mod, norm_g, ffn_w_gate, ffn_w_up, ffn_w_down, w_in_ab,
           mlstm_conv_w, mlstm_conv_b, mlstm_b_i, mlstm_b_f, mlstm_out_g, gla_w_gk, gla_b_gk, gla_out_g,
           w_out_ab, w_qkv, q_norm_g, k_norm_g, w_o):
    bp, sp, d = x_prompt.shape
    bs, ss, _ = x_sample.shape
    depth = w_mod.shape[0]
    xp = x_prompt.reshape(bp * sp, d)
    xs = x_sample.reshape(bs * ss, d)

    c8 = jnp.concatenate([c_ctx[None, :], c, jnp.zeros((8 - 1 - bs, d), F32)], axis=0)
    mod = _modulation(c8, w_mod, b_mod).reshape(depth, 8, N_MOD, d)

    new_c, new_n, new_m, new_s, new_k, new_v = [], [], [], [], [], []
    for l in range(depth):
        mod_p = mod[l, 0:1]
        mod_s = mod[l, 1:1 + bs]

        ffn_w = [(ffn_w_gate[l, half].astype(BF16), ffn_w_up[l, half].astype(BF16),
                  ffn_w_down[l, half].astype(BF16)) for half in range(2)]

        def ffn_half(x, md, j, half):
            return _ffn(x, md[:, 3 * j:3 * j + 3], norm_g[l, j][None, :], *ffn_w[half])

        xp = ffn_half(xp, mod_p, 0, 0)
        xs = ffn_half(xs, mod_s, 0, 0)
        g_mix = norm_g[l, 1][None, :]
        if l % 2 == 0:
            e = l // 2
            w_in = _permute_w_in(w_in_ab[e])
            w_out = w_out_ab[e].astype(BF16)
            gbias = jnp.zeros((2, 2, HEADS), F32)
            gbias = gbias.at[:, 0].set(mlstm_b_i[e]).at[:, 1].set(mlstm_b_f[e])
            gbias = jnp.pad(gbias.reshape(1, 16), ((0, 0), (0, 128 - 16)))
            wgk = jnp.zeros((2, 128, HEADS * DQK), F32)
            for j in range(2):
                lo = LR_OFF + j * GLA_RANK
                wgk = wgk.at[j, lo:lo + GLA_RANK].set(gla_w_gk[e, j])
            wgk = wgk.astype(BF16)
            bgk = gla_b_gk[e][:, None, :]
            conv_w = mlstm_conv_w[e]
            conv_b = mlstm_conv_b[e][None, :]
            a_gain = mlstm_out_g[e][None, :]
            b_gain = gla_out_g[e][None, :]

            def mixer(x, md, batch, seq, c0, n0, m0, s0, emit_state):
                proj = _proj(x, md[:, 3:6], g_mix, w_in, 1280)
                q, k = _conv_qk(proj, conv_w, conv_b, seq)
                m0r = jnp.broadcast_to(m0[..., None], m0.shape + (128,))
                hm, hg, states = [], [], []
                for direction in range(2):
                    r = _mlstm_scan(q, k, proj, gbias, c0, n0, m0r, batch=batch, seq=seq,
                                    direction=direction, emit_state=emit_state)
                    hm.append(r[0])
                    r2 = _gla_scan(proj, wgk, bgk, s0, batch=batch, seq=seq,
                                   direction=direction, emit_state=emit_state)
                    hg.append(r2[0])
                    if emit_state:
                        states.append((r[1], r[2], r[3][..., 0], r2[1]))
                y = _mixer_out(x, md[:, 5:6], hm[0], hm[1], hg[0], hg[1], proj, a_gain, b_gain, w_out)
                return y, states

            zc = jnp.zeros((bp, 2, HEADS, DQK, DV), F32)
            zn = jnp.zeros((bp, 2, HEADS, DQK), F32)
            zm = jnp.zeros((bp, 2, HEADS), F32)
            xp, st = mixer(xp, mod_p, bp, sp, zc, zn, zm, zc, True)
            xs, _ = mixer(xs, mod_s, bs, ss, state_mlstm_C[:, e], state_mlstm_n[:, e],
                          state_mlstm_m[:, e], state_gla_S[:, e], False)
            for lst, idx in ((new_c, 0), (new_n, 1), (new_m, 2), (new_s, 3)):
                lst.append(jnp.stack([st[0][idx], st[1][idx]], axis=1))
        else:
            o = l // 2
            wq = w_qkv[o].astype(BF16)
            wo = w_o[o].astype(BF16)
            qg = q_norm_g[o][None, :]
            kg = k_norm_g[o][None, :]
            nk = HEADS * HEAD_DIM
            proj_p = _proj(xp, mod_p[:, 3:6], g_mix, wq, 1024)
            q, k, v, kf = _qk_prep(proj_p, qg, kg, None)
            a = _attention(q, k, v, batch=bp, sq=sp, sk=sp)
            xp = _attn_out(xp, mod_p[:, 5:6], a, wo)
            new_k.append(kf.reshape(bp, sp, HEADS, HEAD_DIM))
            new_v.append(proj_p[:, N_Q_HEADS * HEAD_DIM + nk:].reshape(bp, sp, HEADS, HEAD_DIM))
            proj_s = _proj(xs, mod_s[:, 3:6], g_mix, wq, 1024)
            q, k, v = _qk_prep(proj_s, qg, kg, _rope_tables(ss))
            past = cache_k.shape[2]
            ck = cache_k[:, o].reshape(bs, past, nk).astype(BF16)
            cv = cache_v[:, o].reshape(bs, past, nk).astype(BF16)
            k_all = jnp.concatenate([ck, k.reshape(bs, ss, nk)], axis=1).reshape(bs * (past + ss), nk)
            v_all = jnp.concatenate([cv, v.reshape(bs, ss, nk)], axis=1).reshape(bs * (past + ss), nk)
            a = _attention(q, k_all, v_all, batch=bs, sq=ss, sk=past + ss)
            xs = _attn_out(xs, mod_s[:, 5:6], a, wo)
        xp = ffn_half(xp, mod_p, 2, 1)
        xs = ffn_half(xs, mod_s, 2, 1)

    return (xp.reshape(bp, sp, d), xs.reshape(bs, ss, d),
            jnp.stack(new_c, axis=1), jnp.stack(new_n, axis=1), jnp.stack(new_m, axis=1),
            jnp.stack(new_s, axis=1), jnp.stack(new_k, axis=1), jnp.stack(new_v, axis=1))
```

```python
import functools

import jax
import jax.numpy as jnp
from jax import lax
from jax.experimental import pallas as pl
from jax.experimental.pallas import tpu as pltpu

F32 = jnp.float32
BF16 = jnp.bfloat16
HIGHEST = lax.Precision.HIGHEST

D_MODEL = 2048
N_MOD = 9
D_FF = 5632
RMS_EPS = 1e-6
ROWS_PER_MOD = 4096
HEADS = 4
DQK = 128
DV = 256
GLA_RANK = 16
GLA_GATE_NORM = 16.0
CONV_W = 3
N_Q_HEADS = 16
Q_PER_KV = 4
HEAD_DIM = 128
GRID_W = 64
ROPE_AXIS = HEAD_DIM // 2
ROPE_THETA = 10000.0

COL_AQ, COL_AK, COL_AV, COL_AO = 0, 512, 1024, 2048
COL_BQ, COL_BK, COL_BV, COL_BG = 3072, 3584, 4096, 5120
COL_SMALL = 6144
AB_PAD_COLS = 6400
LR_OFF = 16

VMEM_LIMIT = 60 * 1024 * 1024
SCAN_L = 128
FFN_TM, FFN_TF = 1024, 512
PROJ_TM = 1024
ATTN_OUT_TM = 512
OUT_TM = 256
PREP_TR = 256
ATT_TQ = 128
ATT_KVG = 2
ATT_KC = 256
LOG2E = 1.4426950408889634


def _cparams(sem):
    return pltpu.CompilerParams(dimension_semantics=sem, vmem_limit_bytes=VMEM_LIMIT)


def _log_sigmoid(x):
    return jnp.minimum(x, 0.0) - jnp.log1p(jnp.exp(-jnp.abs(x)))


def _adaln(x, g, shift, scale):
    ms = jnp.mean(x * x, axis=-1, keepdims=True)
    return (x * lax.rsqrt(ms + RMS_EPS) * g) * (1.0 + scale) + shift


def _mod_index(tm):
    return lambda i, *_: ((i * tm) // ROWS_PER_MOD, 0, 0)


def _mod_kernel(c_ref, w_ref, b_ref, o_ref):
    c = c_ref[...]
    s = (c * jax.nn.sigmoid(c)).astype(BF16)
    o_ref[...] = jnp.dot(s, w_ref[...].astype(BF16), preferred_element_type=F32) + b_ref[...]


def _modulation(c8, w_mod, b_mod):
    depth, d, n = w_mod.shape
    tn = 1024
    return pl.pallas_call(
        _mod_kernel,
        grid=(depth, n // tn),
        in_specs=[
            pl.BlockSpec((8, d), lambda l, j: (0, 0)),
            pl.BlockSpec((None, d, tn), lambda l, j: (l, 0, j)),
            pl.BlockSpec((None, 1, tn), lambda l, j: (l, 0, j)),
        ],
        out_specs=pl.BlockSpec((None, 8, tn), lambda l, j: (l, 0, j)),
        out_shape=jax.ShapeDtypeStruct((depth, 8, n), F32),
        compiler_params=_cparams(("arbitrary", "arbitrary")),
        name="modulation",
    )(c8, w_mod, b_mod.reshape(depth, 1, n))


def _ffn_kernel(x_ref, mod_ref, g_ref, wg_ref, wu_ref, wd_ref, o_ref, h_ref, *, nf):
    f = pl.program_id(1)

    def swiglu_part(h):
        gate = jnp.dot(h, wg_ref[...], preferred_element_type=F32)
        up = jnp.dot(h, wu_ref[...], preferred_element_type=F32)
        a = (gate * jax.nn.sigmoid(gate) * up).astype(BF16)
        return jnp.dot(a, wd_ref[...], preferred_element_type=F32)

    @pl.when(f == 0)
    def _():
        h = _adaln(x_ref[...], g_ref[...], mod_ref[0:1, :], mod_ref[1:2, :]).astype(BF16)
        h_ref[...] = h
        o_ref[...] = swiglu_part(h)

    @pl.when(jnp.logical_and(f > 0, f < nf - 1))
    def _():
        o_ref[...] += swiglu_part(h_ref[...])

    @pl.when(f == nf - 1)
    def _():
        acc = o_ref[...] + swiglu_part(h_ref[...])
        o_ref[...] = x_ref[...] + (0.5 * mod_ref[2:3, :]) * acc


def _ffn(x, mod3, g, wg, wu, wd, layer, half):
    m, d = x.shape
    tm, tf = FFN_TM, FFN_TF
    nf = D_FF // tf
    return pl.pallas_call(
        functools.partial(_ffn_kernel, nf=nf),
        grid=(m // tm, nf),
        in_specs=[
            pl.BlockSpec((tm, d), lambda i, f: (i, 0)),
            pl.BlockSpec((None, 3, d), _mod_index(tm)),
            pl.BlockSpec((1, d), lambda i, f: (0, 0)),
            pl.BlockSpec((None, None, d, tf), lambda i, f: (layer, half, 0, f)),
            pl.BlockSpec((None, None, d, tf), lambda i, f: (layer, half, 0, f)),
            pl.BlockSpec((None, None, tf, d), lambda i, f: (layer, half, f, 0)),
        ],
        out_specs=pl.BlockSpec((tm, d), lambda i, f: (i, 0)),
        out_shape=jax.ShapeDtypeStruct((m, d), F32),
        scratch_shapes=[pltpu.VMEM((tm, d), BF16)],
        compiler_params=_cparams(("parallel", "arbitrary")),
        name="ffn",
    )(x, mod3, g, wg, wu, wd)


def _proj_kernel(x_ref, mod_ref, g_ref, w_ref, o_ref, h_ref):
    j = pl.program_id(1)

    @pl.when(j == 0)
    def _():
        h = _adaln(x_ref[...], g_ref[...], mod_ref[0:1, :], mod_ref[1:2, :]).astype(BF16)
        h_ref[...] = h
        o_ref[...] = jnp.dot(h, w_ref[...], preferred_element_type=F32)

    @pl.when(j > 0)
    def _():
        o_ref[...] = jnp.dot(h_ref[...], w_ref[...], preferred_element_type=F32)


def _proj(x, mod3, g, w, tn):
    m, d = x.shape
    n = w.shape[1]
    tm = PROJ_TM
    return pl.pallas_call(
        _proj_kernel,
        grid=(m // tm, n // tn),
        in_specs=[
            pl.BlockSpec((tm, d), lambda i, j: (i, 0)),
            pl.BlockSpec((None, 3, d), _mod_index(tm)),
            pl.BlockSpec((1, d), lambda i, j: (0, 0)),
            pl.BlockSpec((d, tn), lambda i, j: (0, j)),
        ],
        out_specs=pl.BlockSpec((tm, tn), lambda i, j: (i, j)),
        out_shape=jax.ShapeDtypeStruct((m, n), F32),
        scratch_shapes=[pltpu.VMEM((tm, d), BF16)],
        compiler_params=_cparams(("parallel", "arbitrary")),
        name="adaln_proj",
    )(x, mod3, g, w)


def _conv_kernel(x_ref, prev_ref, next_ref, w_ref, b_ref, q_ref, k_ref, *, tr, seq):
    i = pl.program_id(0)
    x = x_ref[...]
    first = (i * tr) % seq == 0
    last = ((i + 1) * tr) % seq == 0
    prev_row = jnp.where(first, 0.0, prev_ref[7:8, :])
    next_row = jnp.where(last, 0.0, next_ref[0:1, :])
    ridx = lax.broadcasted_iota(jnp.int32, x.shape, 0)
    xm = jnp.where(ridx == 0, prev_row, pltpu.roll(x, 1, 0))
    xp = jnp.where(ridx == tr - 1, next_row, pltpu.roll(x, tr - 1, 0))
    y = xm * w_ref[0:1, :] + x * w_ref[1:2, :] + xp * w_ref[2:3, :] + b_ref[...]
    y = y * jax.nn.sigmoid(y)
    hqk = HEADS * DQK
    q_ref[...] = y[:, :hqk].astype(BF16)
    k_ref[...] = (y[:, hqk:] * (DQK ** -0.5)).astype(BF16)


def _conv_qk(proj, conv_w, conv_b, seq):
    m = proj.shape[0]
    tr = PREP_TR
    c = 2 * HEADS * DQK
    nb8 = m // 8
    return pl.pallas_call(
        functools.partial(_conv_kernel, tr=tr, seq=seq),
        grid=(m // tr,),
        in_specs=[
            pl.BlockSpec((tr, c), lambda i: (i, 0)),
            pl.BlockSpec((8, c), lambda i: (jnp.maximum(i * (tr // 8) - 1, 0), 0)),
            pl.BlockSpec((8, c), lambda i: (jnp.minimum((i + 1) * (tr // 8), nb8 - 1), 0)),
            pl.BlockSpec((CONV_W, c), lambda i: (0, 0)),
            pl.BlockSpec((1, c), lambda i: (0, 0)),
        ],
        out_specs=[
            pl.BlockSpec((tr, c // 2), lambda i: (i, 0)),
            pl.BlockSpec((tr, c // 2), lambda i: (i, 0)),
        ],
        out_shape=[jax.ShapeDtypeStruct((m, c // 2), BF16)] * 2,
        compiler_params=_cparams(("parallel",)),
        name="mlstm_conv",
    )(proj, proj, proj, conv_w, conv_b)


def _order_mask(n, rev):
    t = lax.broadcasted_iota(jnp.int32, (n, n), 0)
    s = lax.broadcasted_iota(jnp.int32, (n, n), 1)
    return (s >= t) if rev else (s <= t)


def _lane_select(cols):
    r = lax.broadcasted_iota(jnp.int32, (128, 128), 0)
    return jnp.concatenate([(r == c).astype(F32) for c in cols], axis=1)


def _dot_exact(a, b):
    return jnp.dot(a, b, precision=HIGHEST, preferred_element_type=F32)


def _dot_nt(a, b):
    return lax.dot_general(a, b, (((1,), (1,)), ((), ())), preferred_element_type=F32)


def _dot_tn(a, b):
    return lax.dot_general(a, b, (((0,), (0,)), ((), ())), preferred_element_type=F32)


def _wide(x):
    return jnp.concatenate([x, x], axis=1)


def _mlstm_chain(q_ref, k_ref, v_ref, g_ref, gb_ref, h_ref, c_ref, n_ref, m_ref, *, direction):
    L = SCAN_L
    rev = direction == 1
    mask = _order_mask(L, rev)
    gates = g_ref[...] + gb_ref[...]
    logf = _log_sigmoid(gates)
    bsum = _dot_exact(mask.astype(F32), logf)
    col_i = [8 * direction + h for h in range(HEADS)]
    col_f = [8 * direction + HEADS + h for h in range(HEADS)]
    i_rep = _dot_exact(gates, _lane_select(col_i))
    b_rep = _dot_exact(bsum, _lane_select(col_f))
    gates_t = gates.T
    bsum_t = bsum.T
    end = 0 if rev else L - 1

    for h in range(HEADS):
        hs = slice(h * DQK, (h + 1) * DQK)
        vs = slice(h * DV, (h + 1) * DV)
        b_c = b_rep[:, hs]
        i_c = i_rep[:, hs]
        b_r = bsum_t[col_f[h]:col_f[h] + 1, :]
        i_r = gates_t[col_i[h]:col_i[h] + 1, :]
        m_prev = m_ref[direction, h:h + 1, :]
        n_prev = n_ref[direction, h:h + 1, :]
        dmat = jnp.where(mask, b_c - b_r + i_r, -jnp.inf)
        inter = b_c + m_prev
        m_t = jnp.maximum(inter, jnp.max(dmat, axis=-1, keepdims=True))
        w_intra = jnp.exp(dmat - m_t)
        w_inter = jnp.exp(inter - m_t)
        qh = q_ref[:, hs]
        kh = k_ref[:, hs]
        vh = v_ref[:, vs].astype(BF16)
        s = _dot_nt(qh, kh) * w_intra
        c_old = c_ref[direction, h]
        num = _wide(w_inter) * jnp.dot(qh, c_old.astype(BF16), preferred_element_type=F32)
        num = num + jnp.dot(s.astype(BF16), vh, preferred_element_type=F32)
        qn = jnp.sum(qh.astype(F32) * n_prev, axis=-1, keepdims=True)
        den = w_inter * qn + jnp.sum(s, axis=-1, keepdims=True)
        inv = 1.0 / jnp.maximum(jnp.abs(den), jnp.exp(-m_t))
        h_ref[:, vs] = num * _wide(inv)

        g_end = b_c[end:end + 1, :]
        dec = g_end - b_c + i_c
        m_new = jnp.maximum(g_end + m_prev, jnp.max(dec, axis=0, keepdims=True))
        ws = jnp.exp(dec - m_new)
        wc = jnp.exp(g_end + m_prev - m_new)
        kw = kh.astype(F32) * ws
        c_ref[direction, h] = _wide(wc) * c_old + _dot_tn(kw.astype(BF16), vh)
        n_ref[direction, h:h + 1, :] = wc * n_prev + jnp.sum(kw, axis=0, keepdims=True)
        m_ref[direction, h:h + 1, :] = m_new


def _gla_chain(q_ref, k_ref, v_ref, g_ref, wgk_ref, bgk_ref, o_ref, s_ref, *, direction):
    L = SCAN_L
    H = L // 2
    rev = direction == 1
    glogit = jnp.dot(g_ref[...].astype(BF16), wgk_ref[direction], preferred_element_type=F32)
    la = _log_sigmoid(glogit + bgk_ref[direction]) * (1.0 / GLA_GATE_NORM)
    bc = _dot_exact(_order_mask(L, rev).astype(F32), la)
    mask_h = _order_mask(H, rev)
    first = slice(H, L) if rev else slice(0, H)
    second = slice(0, H) if rev else slice(H, L)
    first_end = H if rev else H - 1
    end = 0 if rev else L - 1
    sel_end = (lax.broadcasted_iota(jnp.int32, (L, 128), 0) == end).astype(F32)
    scale = DQK ** -0.5

    def scores(qx, kx, bq, bk, anchor):
        qt = (qx * jnp.exp(bq - anchor)).astype(BF16)
        kt = (kx * jnp.exp(anchor - bk)).astype(BF16)
        return _dot_nt(qt, kt)

    for h in range(HEADS):
        hs = slice(h * DQK, (h + 1) * DQK)
        vs = slice(h * DV, (h + 1) * DV)
        bch = bc[:, hs]
        qh = q_ref[:, hs] * scale
        kh = k_ref[:, hs]
        vh = v_ref[:, vs].astype(BF16)
        s_old = s_ref[direction, h]
        o = jnp.dot((qh * jnp.exp(bch)).astype(BF16), s_old.astype(BF16), preferred_element_type=F32)

        def diag(rows):
            mid = rows.start + H // 2
            a = scores(qh[rows], kh[rows], bch[rows], bch[rows], bch[mid:mid + 1, :])
            return jnp.where(mask_h, a, 0.0).astype(BF16)

        a_ff = diag(first)
        a_ss = diag(second)
        a_sf = scores(qh[second], kh[first], bch[second], bch[first],
                      bch[first_end:first_end + 1, :]).astype(BF16)
        o_ref[first, vs] = o[first] + jnp.dot(a_ff, vh[first], preferred_element_type=F32)
        o_ref[second, vs] = (o[second] + jnp.dot(a_sf, vh[first], preferred_element_type=F32)
                             + jnp.dot(a_ss, vh[second], preferred_element_type=F32))

        b_end = bch[end:end + 1, :]
        kdec = (kh * jnp.exp(b_end - bch)).astype(BF16)
        carry = jnp.exp(_dot_exact(bch.T, sel_end))
        s_ref[direction, h] = _wide(carry) * s_old + _dot_tn(kdec, vh)


def _scan_kernel(*refs, nc, emit_state):
    per_dir = [refs[0:7], refs[7:14]]
    gb_ref, wgk_ref, bgk_ref, c0_ref, n0_ref, m0_ref, s0_ref = refs[14:21]
    hm_refs = refs[21:23]
    hg_refs = refs[23:25]
    rest = refs[25:]
    if emit_state:
        cf_ref, nf_ref, mf_ref, sf_ref = rest[:4]
        rest = rest[4:]
    c_ref, n_ref, m_ref, s_ref = rest
    c = pl.program_id(1)

    @pl.when(c == 0)
    def _():
        c_ref[...] = c0_ref[...]
        n_ref[...] = n0_ref[...]
        m_ref[...] = m0_ref[...]
        s_ref[...] = s0_ref[...]

    for direction in range(2):
        mq, mk, mv, small, gq, gk, gv = per_dir[direction]
        _mlstm_chain(mq, mk, mv, small, gb_ref, hm_refs[direction], c_ref, n_ref, m_ref, direction=direction)
        _gla_chain(gq, gk, gv, small, wgk_ref, bgk_ref, hg_refs[direction], s_ref, direction=direction)

    if emit_state:
        @pl.when(c == nc - 1)
        def _():
            cf_ref[...] = c_ref[...]
            nf_ref[...] = n_ref[...]
            mf_ref[...] = m_ref[...]
            sf_ref[...] = s_ref[...]


def _mixer_scan(q, k, proj, gbias, wgk, bgk, c0, n0, m0, s0, *, batch, seq, emit_state):
    L = SCAN_L
    nc = seq // L
    m = batch * seq
    qk_w, v_w = HEADS * DQK, HEADS * DV

    def dir_specs(rev):
        def row(b, c):
            return b * nc + (nc - 1 - c if rev else c)
        spec = lambda w, col: pl.BlockSpec((L, w), lambda b, c: (row(b, c), col))
        ins = [spec(qk_w, 0), spec(qk_w, 0), spec(v_w, COL_AV // v_w), spec(128, COL_SMALL // 128),
               spec(qk_w, COL_BQ // qk_w), spec(qk_w, COL_BK // qk_w), spec(v_w, COL_BV // v_w)]
        return ins, spec(v_w, 0)

    ins_f, out_f = dir_specs(False)
    ins_r, out_r = dir_specs(True)
    state = lambda *tail: pl.BlockSpec((None, 2, HEADS) + tail, lambda b, c: (b, 0, 0) + (0,) * len(tail))
    in_specs = ins_f + ins_r + [
        pl.BlockSpec((1, 128), lambda b, c: (0, 0)),
        pl.BlockSpec((2, 128, qk_w), lambda b, c: (0, 0, 0)),
        pl.BlockSpec((2, 1, qk_w), lambda b, c: (0, 0, 0)),
        state(DQK, DV), state(DQK), state(128), state(DQK, DV),
    ]
    out_specs = [out_f, out_r, out_f, out_r]
    out_shape = [jax.ShapeDtypeStruct((m, v_w), F32)] * 4
    state_shapes = [(batch, 2, HEADS, DQK, DV), (batch, 2, HEADS, DQK), (batch, 2, HEADS, 128),
                    (batch, 2, HEADS, DQK, DV)]
    if emit_state:
        out_specs += [state(DQK, DV), state(DQK), state(128), state(DQK, DV)]
        out_shape += [jax.ShapeDtypeStruct(sh, F32) for sh in state_shapes]
    per_dir_args = [q, k, proj, proj, proj, proj, proj]
    return pl.pallas_call(
        functools.partial(_scan_kernel, nc=nc, emit_state=emit_state),
        grid=(batch, nc),
        in_specs=in_specs,
        out_specs=out_specs,
        out_shape=out_shape,
        scratch_shapes=[pltpu.VMEM(sh[1:], F32) for sh in state_shapes],
        compiler_params=_cparams(("parallel", "arbitrary")),
        name="mixer_scan",
    )(*per_dir_args, *per_dir_args, gbias, wgk, bgk, c0, n0, m0, s0)


def _head_rms(x):
    parts = []
    for h in range(HEADS):
        xh = x[:, h * DV:(h + 1) * DV]
        ms = jnp.mean(xh * xh, axis=-1, keepdims=True)
        parts.append(xh * lax.rsqrt(ms + RMS_EPS))
    return jnp.concatenate(parts, axis=1)


def _mixout_kernel(x_ref, mod_ref, hmf_ref, hmr_ref, hgf_ref, hgr_ref, ao_ref, bg_ref,
                   ag_ref, bgn_ref, w_ref, o_ref):
    ya = _head_rms(hmf_ref[...] + hmr_ref[...]) * ag_ref[...] * jax.nn.sigmoid(ao_ref[...])
    bg = bg_ref[...]
    yb = _head_rms(hgf_ref[...] + hgr_ref[...]) * bgn_ref[...] * (bg * jax.nn.sigmoid(bg))
    hw = HEADS * DV
    y = jnp.dot(ya.astype(BF16), w_ref[0:hw, :], preferred_element_type=F32)
    y = y + jnp.dot(yb.astype(BF16), w_ref[hw:2 * hw, :], preferred_element_type=F32)
    o_ref[...] = x_ref[...] + mod_ref[...] * y


def _mixer_out(x, gate, hmf, hmr, hgf, hgr, proj, a_gain, b_gain, w_out):
    m, d = x.shape
    tm = OUT_TM
    hw = HEADS * DV
    wide = lambda col: pl.BlockSpec((tm, hw), lambda i: (i, col))
    return pl.pallas_call(
        _mixout_kernel,
        grid=(m // tm,),
        in_specs=[
            pl.BlockSpec((tm, d), lambda i: (i, 0)),
            pl.BlockSpec((None, 1, d), _mod_index(tm)),
            wide(0), wide(0), wide(0), wide(0),
            wide(COL_AO // hw), wide(COL_BG // hw),
            pl.BlockSpec((1, hw), lambda i: (0, 0)),
            pl.BlockSpec((1, hw), lambda i: (0, 0)),
            pl.BlockSpec((2 * hw, d), lambda i: (0, 0)),
        ],
        out_specs=pl.BlockSpec((tm, d), lambda i: (i, 0)),
        out_shape=jax.ShapeDtypeStruct((m, d), F32),
        compiler_params=_cparams(("parallel",)),
        name="mixer_out",
    )(x, gate, hmf, hmr, hgf, hgr, proj, proj, a_gain, b_gain, w_out)


def _rope(x, cos, sin):
    n = x.shape[1]
    lane = lax.broadcasted_iota(jnp.int32, x.shape, 1)
    partner = jnp.where((lane % 64) < 32, pltpu.roll(x, n - 32, 1), pltpu.roll(x, 32, 1))
    reps = n // HEAD_DIM
    cos_t = jnp.concatenate([cos] * reps, axis=1) if reps > 1 else cos
    sin_t = jnp.concatenate([sin] * reps, axis=1) if reps > 1 else sin
    return x * cos_t + partner * sin_t


def _head_norm(x, g, n_heads):
    parts = []
    for h in range(n_heads):
        xh = x[:, h * HEAD_DIM:(h + 1) * HEAD_DIM]
        ms = jnp.mean(xh * xh, axis=-1, keepdims=True)
        parts.append(xh * lax.rsqrt(ms + RMS_EPS) * g)
    return jnp.concatenate(parts, axis=1)


def _qkprep_kernel(q_in, k_in, v_in, qg_ref, kg_ref, *rest, rope):
    if rope:
        cos_ref, sin_ref, q_ref, k_ref, v_ref = rest
    else:
        q_ref, k_ref, v_ref, kf_ref = rest
    q = _head_norm(q_in[...], qg_ref[...], N_Q_HEADS)
    k = _head_norm(k_in[...], kg_ref[...], HEADS)
    if rope:
        q = _rope(q, cos_ref[...], sin_ref[...])
        k = _rope(k, cos_ref[...], sin_ref[...])
    else:
        kf_ref[...] = k
    q_ref[...] = (q * (HEAD_DIM ** -0.5 * LOG2E)).astype(BF16)
    k_ref[...] = k.astype(BF16)
    v_ref[...] = v_in[...].astype(BF16)


def _qk_prep(proj, qg, kg, rope_tabs):
    m = proj.shape[0]
    tr = PREP_TR
    nq = N_Q_HEADS * HEAD_DIM
    nk = HEADS * HEAD_DIM
    rope = rope_tabs is not None
    in_specs = [
        pl.BlockSpec((tr, nq), lambda i: (i, 0)),
        pl.BlockSpec((tr, nk), lambda i: (i, nq // nk)),
        pl.BlockSpec((tr, nk), lambda i: (i, nq // nk + 1)),
        pl.BlockSpec((1, HEAD_DIM), lambda i: (0, 0)),
        pl.BlockSpec((1, HEAD_DIM), lambda i: (0, 0)),
    ]
    args = [proj, proj, proj, qg, kg]
    out_specs = [
        pl.BlockSpec((tr, nq), lambda i: (i, 0)),
        pl.BlockSpec((tr, nk), lambda i: (i, 0)),
        pl.BlockSpec((tr, nk), lambda i: (i, 0)),
    ]
    out_shape = [
        jax.ShapeDtypeStruct((m, nq), BF16),
        jax.ShapeDtypeStruct((m, nk), BF16),
        jax.ShapeDtypeStruct((m, nk), BF16),
    ]
    if rope:
        nt = rope_tabs[0].shape[0] // tr
        in_specs += [pl.BlockSpec((tr, HEAD_DIM), lambda i: (i % nt, 0))] * 2
        args += list(rope_tabs)
    else:
        out_specs.append(pl.BlockSpec((tr, nk), lambda i: (i, 0)))
        out_shape.append(jax.ShapeDtypeStruct((m, nk), F32))
    return pl.pallas_call(
        functools.partial(_qkprep_kernel, rope=rope),
        grid=(m // tr,),
        in_specs=in_specs,
        out_specs=out_specs,
        out_shape=out_shape,
        compiler_params=_cparams(("parallel",)),
        name="qk_prep",
    )(*args)


def _attn_kernel(q_ref, k_ref, v_ref, o_ref, sc_ref, *, tq, sk):
    kc = min(ATT_KC, sk)
    chunks = [slice(j * kc, (j + 1) * kc) for j in range(sk // kc)]
    n_chains = ATT_KVG * Q_PER_KV
    head = lambda j: slice(j * HEAD_DIM, (j + 1) * HEAD_DIM)
    kv = lambda j: head(j // Q_PER_KV)

    def scores_pass(j):
        qj = q_ref[:, head(j)]
        mrun = None
        for ch in chunks:
            s = _dot_nt(qj, k_ref[ch, kv(j)])
            sc_ref[j, :, ch] = s
            for b in range(kc // 128):
                sb = s[:, b * 128:(b + 1) * 128]
                mrun = sb if mrun is None else jnp.maximum(mrun, sb)
        return jnp.max(mrun, axis=-1, keepdims=True)

    def values_pass(j, m):
        lrun = jnp.zeros((tq, 128), F32)
        acc = jnp.zeros((tq, HEAD_DIM), F32)
        for ch in chunks:
            p = jnp.exp2(sc_ref[j, :, ch] - m)
            for b in range(kc // 128):
                lrun = lrun + p[:, b * 128:(b + 1) * 128]
            acc = acc + jnp.dot(p.astype(BF16), v_ref[ch, kv(j)], preferred_element_type=F32)
        o = acc * (1.0 / jnp.sum(lrun, axis=-1, keepdims=True))
        o_ref[:, head(j)] = o.astype(BF16)

    m_prev = scores_pass(0)
    for j in range(1, n_chains):
        m_next = scores_pass(j)
        values_pass(j - 1, m_prev)
        m_prev = m_next
    values_pass(n_chains - 1, m_prev)


def _attention(q, k, v, *, batch, sq, sk):
    tq = ATT_TQ
    nq = sq // tq
    gw = ATT_KVG * Q_PER_KV * HEAD_DIM
    kw = ATT_KVG * HEAD_DIM
    return pl.pallas_call(
        functools.partial(_attn_kernel, tq=tq, sk=sk),
        scratch_shapes=[pltpu.VMEM((ATT_KVG * Q_PER_KV, tq, sk), F32)],
        grid=(batch, HEADS // ATT_KVG, nq),
        in_specs=[
            pl.BlockSpec((tq, gw), lambda b, g, i: (b * nq + i, g)),
            pl.BlockSpec((sk, kw), lambda b, g, i: (b, g)),
            pl.BlockSpec((sk, kw), lambda b, g, i: (b, g)),
        ],
        out_specs=pl.BlockSpec((tq, gw), lambda b, g, i: (b * nq + i, g)),
        out_shape=jax.ShapeDtypeStruct((batch * sq, N_Q_HEADS * HEAD_DIM), BF16),
        compiler_params=_cparams(("parallel", "parallel", "arbitrary")),
        name="gqa_attention",
    )(q, k, v)


def _oproj_kernel(x_ref, mod_ref, a_ref, w_ref, o_ref):
    y = jnp.dot(a_ref[...], w_ref[...], preferred_element_type=F32)
    o_ref[...] = x_ref[...] + mod_ref[...] * y


def _attn_out(x, gate, a, w_o):
    m, d = x.shape
    tm = ATTN_OUT_TM
    return pl.pallas_call(
        _oproj_kernel,
        grid=(m // tm,),
        in_specs=[
            pl.BlockSpec((tm, d), lambda i: (i, 0)),
            pl.BlockSpec((None, 1, d), _mod_index(tm)),
            pl.BlockSpec((tm, a.shape[1]), lambda i: (i, 0)),
            pl.BlockSpec(w_o.shape, lambda i: (0, 0)),
        ],
        out_specs=pl.BlockSpec((tm, d), lambda i: (i, 0)),
        out_shape=jax.ShapeDtypeStruct((m, d), F32),
        compiler_params=_cparams(("parallel",)),
        name="attn_out",
    )(x, gate, a, w_o)


def _permute_w_in(w):
    a_g = w[:, 3072:3088]
    main = jnp.concatenate([w[:, :3072], w[:, 3088:6160]], axis=1)
    small = jnp.concatenate([a_g, w[:, 6160:6192]], axis=1)
    pad = jnp.zeros((w.shape[0], AB_PAD_COLS - 6192), w.dtype)
    return jnp.concatenate([main, small, pad], axis=1).astype(BF16)


def _rope_tables(n_tokens):
    t = jnp.arange(n_tokens)
    row = (t // GRID_W).astype(F32)
    col = (t % GRID_W).astype(F32)
    inv = ROPE_THETA ** (-jnp.arange(0, ROPE_AXIS, 2, dtype=F32) / ROPE_AXIS)
    ar = row[:, None] * inv
    ac = col[:, None] * inv
    cos = jnp.concatenate([jnp.cos(ar), jnp.cos(ar), jnp.cos(ac), jnp.cos(ac)], axis=1)
    sin = jnp.concatenate([-jnp.sin(ar), jnp.sin(ar), -jnp.sin(ac), jnp.sin(ac)], axis=1)
    return cos, sin


def kernel(x_prompt, x_sample, c, c_ctx, state_mlstm_C, state_mlstm_n, state_mlstm_m, state_gla_S,
           cache_k, cache_v, w_mod, b_mod, norm_g, ffn_w_gate, ffn_w_up, ffn_w_down, w_in_ab,
           mlstm_conv_w, mlstm_conv_b, mlstm_b_i, mlstm_b_f, mlstm_out_g, gla_w_gk, gla_b_gk, gla_out_g,
           w_out_ab, w_qkv, q_norm_g, k_norm_g, w_o):
    bp, sp, d = x_prompt.shape
    bs, ss, _ = x_sample.shape
    depth = w_mod.shape[0]
    xp = x_prompt.reshape(bp * sp, d)
    xs = x_sample.reshape(bs * ss, d)

    c8 = jnp.concatenate([c_ctx[None, :], c, jnp.zeros((8 - 1 - bs, d), F32)], axis=0)
    mod = _modulation(c8, w_mod, b_mod).reshape(depth, 8, N_MOD, d)

    ffn_w = (ffn_w_gate.astype(BF16), ffn_w_up.astype(BF16), ffn_w_down.astype(BF16))
    new_c, new_n, new_m, new_s, new_k, new_v = [], [], [], [], [], []
    for l in range(depth):
        mod_p = mod[l, 0:1]
        mod_s = mod[l, 1:1 + bs]

        def ffn_half(x, md, j, half):
            return _ffn(x, md[:, 3 * j:3 * j + 3], norm_g[l, j][None, :], *ffn_w, l, half)

        xp = ffn_half(xp, mod_p, 0, 0)
        xs = ffn_half(xs, mod_s, 0, 0)
        g_mix = norm_g[l, 1][None, :]
        if l % 2 == 0:
            e = l // 2
            w_in = _permute_w_in(w_in_ab[e])
            w_out = w_out_ab[e].astype(BF16)
            gbias = jnp.zeros((2, 2, HEADS), F32)
            gbias = gbias.at[:, 0].set(mlstm_b_i[e]).at[:, 1].set(mlstm_b_f[e])
            gbias = jnp.pad(gbias.reshape(1, 16), ((0, 0), (0, 128 - 16)))
            wgk = jnp.zeros((2, 128, HEADS * DQK), F32)
            for j in range(2):
                lo = LR_OFF + j * GLA_RANK
                wgk = wgk.at[j, lo:lo + GLA_RANK].set(gla_w_gk[e, j])
            wgk = wgk.astype(BF16)
            bgk = gla_b_gk[e][:, None, :]
            conv_w = mlstm_conv_w[e]
            conv_b = mlstm_conv_b[e][None, :]
            a_gain = mlstm_out_g[e][None, :]
            b_gain = gla_out_g[e][None, :]

            def mixer(x, md, batch, seq, c0, n0, m0, s0, emit_state):
                proj = _proj(x, md[:, 3:6], g_mix, w_in, 1280)
                q, k = _conv_qk(proj, conv_w, conv_b, seq)
                m0r = jnp.broadcast_to(m0[..., None], m0.shape + (128,))
                r = _mixer_scan(q, k, proj, gbias, wgk, bgk, c0, n0, m0r, s0,
                                batch=batch, seq=seq, emit_state=emit_state)
                y = _mixer_out(x, md[:, 5:6], r[0], r[1], r[2], r[3], proj, a_gain, b_gain, w_out)
                return y, r[4:]

            zc = jnp.zeros((bp, 2, HEADS, DQK, DV), F32)
            zn = jnp.zeros((bp, 2, HEADS, DQK), F32)
            zm = jnp.zeros((bp, 2, HEADS), F32)
            xp, st = mixer(xp, mod_p, bp, sp, zc, zn, zm, zc, True)
            xs, _ = mixer(xs, mod_s, bs, ss, state_mlstm_C[:, e], state_mlstm_n[:, e],
                          state_mlstm_m[:, e], state_gla_S[:, e], False)
            new_c.append(st[0])
            new_n.append(st[1])
            new_m.append(st[2][..., 0])
            new_s.append(st[3])
        else:
            o = l // 2
            wq = w_qkv[o].astype(BF16)
            wo = w_o[o].astype(BF16)
            qg = q_norm_g[o][None, :]
            kg = k_norm_g[o][None, :]
            nk = HEADS * HEAD_DIM
            proj_p = _proj(xp, mod_p[:, 3:6], g_mix, wq, 1024)
            q, k, v, kf = _qk_prep(proj_p, qg, kg, None)
            a = _attention(q, k, v, batch=bp, sq=sp, sk=sp)
            xp = _attn_out(xp, mod_p[:, 5:6], a, wo)
            new_k.append(kf.reshape(bp, sp, HEADS, HEAD_DIM))
            new_v.append(proj_p[:, N_Q_HEADS * HEAD_DIM + nk:].reshape(bp, sp, HEADS, HEAD_DIM))
            proj_s = _proj(xs, mod_s[:, 3:6], g_mix, wq, 1024)
            q, k, v = _qk_prep(proj_s, qg, kg, _rope_tables(ss))
            past = cache_k.shape[2]
            ck = cache_k[:, o].reshape(bs, past, nk).astype(BF16)
            cv = cache_v[:, o].reshape(bs, past, nk).astype(BF16)
            k_all = jnp.concatenate([ck, k.reshape(bs, ss, nk)], axis=1).reshape(bs * (past + ss), nk)
            v_all = jnp.concatenate([cv, v.reshape(bs, ss, nk)], axis=1).reshape(bs * (past + ss), nk)
            a = _attention(q, k_all, v_all, batch=bs, sq=ss, sk=past + ss)
            xs = _attn_out(xs, mod_s[:, 5:6], a, wo)
        xp = ffn_half(xp, mod_p, 2, 1)
        xs = ffn_half(xs, mod_s, 2, 1)

    return (xp.reshape(bp, sp, d), xs.reshape(bs, ss, d),
            jnp.stack(new_c, axis=1), jnp.stack(new_n, axis=1), jnp.stack(new_m, axis=1),
            jnp.stack(new_s, axis=1), jnp.stack(new_k, axis=1), jnp.stack(new_v, axis=1))
```

```python
import functools

import jax
import jax.numpy as jnp
from jax import lax
from jax.experimental import pallas as pl
from jax.experimental.pallas import tpu as pltpu

F32 = jnp.float32
BF16 = jnp.bfloat16

D_MODEL = 2048
N_MOD = 9
D_FF = 5632
RMS_EPS = 1e-6
ROWS_PER_MOD = 4096
HEADS = 4
DQK = 128
DV = 256
GLA_RANK = 16
GLA_GATE_NORM = 16.0
CONV_W = 3
N_Q_HEADS = 16
Q_PER_KV = 4
HEAD_DIM = 128
GRID_W = 64
ROPE_AXIS = HEAD_DIM // 2
ROPE_THETA = 10000.0

COL_AQ, COL_AK, COL_AV, COL_AO = 0, 512, 1024, 2048
COL_BQ, COL_BK, COL_BV, COL_BG = 3072, 3584, 4096, 5120
COL_SMALL = 6144
AB_PAD_COLS = 6400
LR_OFF = 16

VMEM_LIMIT = 60 * 1024 * 1024
SCAN_L = 128
FFN_TM, FFN_TF = 1024, 512
PROJ_TM = 1024
ATTN_OUT_TM = 512
OUT_TM = 256
PREP_TR = 256
ATT_TQ = 128
ATT_KVG = 2
ATT_KC = 256
LOG2E = 1.4426950408889634


def _cparams(sem):
    return pltpu.CompilerParams(dimension_semantics=sem, vmem_limit_bytes=VMEM_LIMIT)


def _log_sigmoid(x):
    return jnp.minimum(x, 0.0) - jnp.log1p(jnp.exp(-jnp.abs(x)))


def _adaln(x, g, shift, scale):
    ms = jnp.mean(x * x, axis=-1, keepdims=True)
    return (x * lax.rsqrt(ms + RMS_EPS) * g) * (1.0 + scale) + shift


def _mod_index(tm):
    return lambda i, *_: ((i * tm) // ROWS_PER_MOD, 0, 0)


def _mod_kernel(c_ref, w_ref, b_ref, o_ref):
    c = c_ref[...]
    s = (c * jax.nn.sigmoid(c)).astype(BF16)
    o_ref[...] = jnp.dot(s, w_ref[...].astype(BF16), preferred_element_type=F32) + b_ref[...]


def _modulation(c8, w_mod, b_mod):
    depth, d, n = w_mod.shape
    tn = 1024
    return pl.pallas_call(
        _mod_kernel,
        grid=(depth, n // tn),
        in_specs=[
            pl.BlockSpec((8, d), lambda l, j: (0, 0)),
            pl.BlockSpec((None, d, tn), lambda l, j: (l, 0, j)),
            pl.BlockSpec((None, 1, tn), lambda l, j: (l, 0, j)),
        ],
        out_specs=pl.BlockSpec((None, 8, tn), lambda l, j: (l, 0, j)),
        out_shape=jax.ShapeDtypeStruct((depth, 8, n), F32),
        compiler_params=_cparams(("arbitrary", "arbitrary")),
        name="modulation",
    )(c8, w_mod, b_mod.reshape(depth, 1, n))


def _ffn_kernel(x_ref, mod_ref, g_ref, wg_ref, wu_ref, wd_ref, o_ref, h_ref, *, nf):
    f = pl.program_id(1)

    def swiglu_part(h):
        gate = jnp.dot(h, wg_ref[...], preferred_element_type=F32)
        up = jnp.dot(h, wu_ref[...], preferred_element_type=F32)
        a = (gate * jax.nn.sigmoid(gate) * up).astype(BF16)
        return jnp.dot(a, wd_ref[...], preferred_element_type=F32)

    @pl.when(f == 0)
    def _():
        h = _adaln(x_ref[...], g_ref[...], mod_ref[0:1, :], mod_ref[1:2, :]).astype(BF16)
        h_ref[...] = h
        o_ref[...] = swiglu_part(h)

    @pl.when(jnp.logical_and(f > 0, f < nf - 1))
    def _():
        o_ref[...] += swiglu_part(h_ref[...])

    @pl.when(f == nf - 1)
    def _():
        acc = o_ref[...] + swiglu_part(h_ref[...])
        o_ref[...] = x_ref[...] + (0.5 * mod_ref[2:3, :]) * acc


def _ffn(x, mod3, g, wg, wu, wd, layer, half):
    m, d = x.shape
    tm, tf = FFN_TM, FFN_TF
    nf = D_FF // tf
    return pl.pallas_call(
        functools.partial(_ffn_kernel, nf=nf),
        grid=(m // tm, nf),
        in_specs=[
            pl.BlockSpec((tm, d), lambda i, f: (i, 0)),
            pl.BlockSpec((None, 3, d), _mod_index(tm)),
            pl.BlockSpec((1, d), lambda i, f: (0, 0)),
            pl.BlockSpec((None, None, d, tf), lambda i, f: (layer, half, 0, f)),
            pl.BlockSpec((None, None, d, tf), lambda i, f: (layer, half, 0, f)),
            pl.BlockSpec((None, None, tf, d), lambda i, f: (layer, half, f, 0)),
        ],
        out_specs=pl.BlockSpec((tm, d), lambda i, f: (i, 0)),
        out_shape=jax.ShapeDtypeStruct((m, d), F32),
        scratch_shapes=[pltpu.VMEM((tm, d), BF16)],
        compiler_params=_cparams(("parallel", "arbitrary")),
        name="ffn",
    )(x, mod3, g, wg, wu, wd)


def _proj_kernel(x_ref, mod_ref, g_ref, w_ref, o_ref, h_ref):
    j = pl.program_id(1)

    @pl.when(j == 0)
    def _():
        h = _adaln(x_ref[...], g_ref[...], mod_ref[0:1, :], mod_ref[1:2, :]).astype(BF16)
        h_ref[...] = h
        o_ref[...] = jnp.dot(h, w_ref[...], preferred_element_type=F32)

    @pl.when(j > 0)
    def _():
        o_ref[...] = jnp.dot(h_ref[...], w_ref[...], preferred_element_type=F32)


def _proj(x, mod3, g, w, tn):
    m, d = x.shape
    n = w.shape[1]
    tm = PROJ_TM
    return pl.pallas_call(
        _proj_kernel,
        grid=(m // tm, n // tn),
        in_specs=[
            pl.BlockSpec((tm, d), lambda i, j: (i, 0)),
            pl.BlockSpec((None, 3, d), _mod_index(tm)),
            pl.BlockSpec((1, d), lambda i, j: (0, 0)),
            pl.BlockSpec((d, tn), lambda i, j: (0, j)),
        ],
        out_specs=pl.BlockSpec((tm, tn), lambda i, j: (i, j)),
        out_shape=jax.ShapeDtypeStruct((m, n), F32),
        scratch_shapes=[pltpu.VMEM((tm, d), BF16)],
        compiler_params=_cparams(("parallel", "arbitrary")),
        name="adaln_proj",
    )(x, mod3, g, w)


def _conv_kernel(x_ref, prev_ref, next_ref, w_ref, b_ref, q_ref, k_ref, *, tr, seq):
    i = pl.program_id(0)
    x = x_ref[...]
    first = (i * tr) % seq == 0
    last = ((i + 1) * tr) % seq == 0
    prev_row = jnp.where(first, 0.0, prev_ref[7:8, :])
    next_row = jnp.where(last, 0.0, next_ref[0:1, :])
    ridx = lax.broadcasted_iota(jnp.int32, x.shape, 0)
    xm = jnp.where(ridx == 0, prev_row, pltpu.roll(x, 1, 0))
    xp = jnp.where(ridx == tr - 1, next_row, pltpu.roll(x, tr - 1, 0))
    y = xm * w_ref[0:1, :] + x * w_ref[1:2, :] + xp * w_ref[2:3, :] + b_ref[...]
    y = y * jax.nn.sigmoid(y)
    hqk = HEADS * DQK
    q_ref[...] = y[:, :hqk].astype(BF16)
    k_ref[...] = (y[:, hqk:] * (DQK ** -0.5)).astype(BF16)


def _conv_qk(proj, conv_w, conv_b, seq):
    m = proj.shape[0]
    tr = PREP_TR
    c = 2 * HEADS * DQK
    nb8 = m // 8
    return pl.pallas_call(
        functools.partial(_conv_kernel, tr=tr, seq=seq),
        grid=(m // tr,),
        in_specs=[
            pl.BlockSpec((tr, c), lambda i: (i, 0)),
            pl.BlockSpec((8, c), lambda i: (jnp.maximum(i * (tr // 8) - 1, 0), 0)),
            pl.BlockSpec((8, c), lambda i: (jnp.minimum((i + 1) * (tr // 8), nb8 - 1), 0)),
            pl.BlockSpec((CONV_W, c), lambda i: (0, 0)),
            pl.BlockSpec((1, c), lambda i: (0, 0)),
        ],
        out_specs=[
            pl.BlockSpec((tr, c // 2), lambda i: (i, 0)),
            pl.BlockSpec((tr, c // 2), lambda i: (i, 0)),
        ],
        out_shape=[jax.ShapeDtypeStruct((m, c // 2), BF16)] * 2,
        compiler_params=_cparams(("parallel",)),
        name="mlstm_conv",
    )(proj, proj, proj, conv_w, conv_b)


def _order_mask(n, rev):
    t = lax.broadcasted_iota(jnp.int32, (n, n), 0)
    s = lax.broadcasted_iota(jnp.int32, (n, n), 1)
    return (s >= t) if rev else (s <= t)


def _masked_sum(mask, x):
    n = x.shape[1]
    hi = x.astype(BF16)
    r = x - hi.astype(F32)
    mid = r.astype(BF16)
    lo = (r - mid.astype(F32)).astype(BF16)
    y = jnp.dot(mask.astype(BF16), jnp.concatenate([hi, mid, lo], axis=1), preferred_element_type=F32)
    return y[:, :n] + y[:, n:2 * n] + y[:, 2 * n:]


def _col_rep(row):
    return jnp.broadcast_to(row, (128, 128)).T


def _dot_nt(a, b):
    return lax.dot_general(a, b, (((1,), (1,)), ((), ())), preferred_element_type=F32)


def _dot_tn(a, b):
    return lax.dot_general(a, b, (((0,), (0,)), ((), ())), preferred_element_type=F32)


def _wide(x):
    return jnp.concatenate([x, x], axis=1)


def _mlstm_chain(q_ref, k_ref, v_ref, g_ref, gb_ref, h_ref, c_ref, n_ref, m_ref, *, direction):
    L = SCAN_L
    rev = direction == 1
    mask = _order_mask(L, rev)
    gates = g_ref[...] + gb_ref[...]
    logf = _log_sigmoid(gates)
    bsum = _masked_sum(mask, logf)
    col_i = [8 * direction + h for h in range(HEADS)]
    col_f = [8 * direction + HEADS + h for h in range(HEADS)]
    gates_t = gates.T
    bsum_t = bsum.T
    end = 0 if rev else L - 1

    def head(h):
        hs = slice(h * DQK, (h + 1) * DQK)
        vs = slice(h * DV, (h + 1) * DV)
        b_r = bsum_t[col_f[h]:col_f[h] + 1, :]
        i_r = gates_t[col_i[h]:col_i[h] + 1, :]
        b_c = jnp.broadcast_to(bsum[:, col_f[h]:col_f[h] + 1], (L, 128))
        i_c = jnp.broadcast_to(gates[:, col_i[h]:col_i[h] + 1], (L, 128))
        m_prev = m_ref[direction, h:h + 1, :]
        n_prev = n_ref[direction, h:h + 1, :]
        dmat = jnp.where(mask, b_c - b_r + i_r, -jnp.inf)
        inter = b_c + m_prev
        m_t = jnp.maximum(inter, jnp.max(dmat, axis=-1, keepdims=True))
        w_intra = jnp.exp(dmat - m_t)
        w_inter = jnp.exp(inter - m_t)
        qh = q_ref[:, hs]
        kh = k_ref[:, hs]
        vh = v_ref[:, vs].astype(BF16)
        s = _dot_nt(qh, kh) * w_intra
        c_old = c_ref[direction, h]
        num = _wide(w_inter) * jnp.dot(qh, c_old.astype(BF16), preferred_element_type=F32)
        num = num + jnp.dot(s.astype(BF16), vh, preferred_element_type=F32)
        qn = jnp.sum(qh.astype(F32) * n_prev, axis=-1, keepdims=True)
        den = w_inter * qn + jnp.sum(s, axis=-1, keepdims=True)
        inv = 1.0 / jnp.maximum(jnp.abs(den), jnp.exp(-m_t))
        h_ref[:, vs] = num * _wide(inv)

        g_end = b_c[end:end + 1, :]
        dec = g_end - b_c + i_c
        m_new = jnp.maximum(g_end + m_prev, jnp.max(dec, axis=0, keepdims=True))
        ws = jnp.exp(dec - m_new)
        wc = jnp.exp(g_end + m_prev - m_new)
        kw = kh.astype(F32) * ws
        c_ref[direction, h] = _wide(wc) * c_old + _dot_tn(kw.astype(BF16), vh)
        n_ref[direction, h:h + 1, :] = wc * n_prev + jnp.sum(kw, axis=0, keepdims=True)
        m_ref[direction, h:h + 1, :] = m_new

    return head


def _gla_chain(q_ref, k_ref, v_ref, g_ref, wgk_ref, bgk_ref, o_ref, s_ref, *, direction):
    L = SCAN_L
    H = L // 2
    rev = direction == 1
    glogit = jnp.dot(g_ref[...].astype(BF16), wgk_ref[direction], preferred_element_type=F32)
    la = _log_sigmoid(glogit + bgk_ref[direction]) * (1.0 / GLA_GATE_NORM)
    bc = _masked_sum(_order_mask(L, rev), la)
    mask_h = _order_mask(H, rev)
    first = slice(H, L) if rev else slice(0, H)
    second = slice(0, H) if rev else slice(H, L)
    first_end = H if rev else H - 1
    end = 0 if rev else L - 1
    scale = DQK ** -0.5

    def scores(qx, kx, bq, bk, anchor):
        qt = (qx * jnp.exp(bq - anchor)).astype(BF16)
        kt = (kx * jnp.exp(anchor - bk)).astype(BF16)
        return _dot_nt(qt, kt)

    def head(h):
        hs = slice(h * DQK, (h + 1) * DQK)
        vs = slice(h * DV, (h + 1) * DV)
        bch = bc[:, hs]
        qh = q_ref[:, hs] * scale
        kh = k_ref[:, hs]
        vh = v_ref[:, vs].astype(BF16)
        s_old = s_ref[direction, h]
        o = jnp.dot((qh * jnp.exp(bch)).astype(BF16), s_old.astype(BF16), preferred_element_type=F32)

        def diag(rows):
            mid = rows.start + H // 2
            a = scores(qh[rows], kh[rows], bch[rows], bch[rows], bch[mid:mid + 1, :])
            return jnp.where(mask_h, a, 0.0).astype(BF16)

        a_ff = diag(first)
        a_ss = diag(second)
        a_sf = scores(qh[second], kh[first], bch[second], bch[first],
                      bch[first_end:first_end + 1, :]).astype(BF16)
        o_ref[first, vs] = o[first] + jnp.dot(a_ff, vh[first], preferred_element_type=F32)
        o_ref[second, vs] = (o[second] + jnp.dot(a_sf, vh[first], preferred_element_type=F32)
                             + jnp.dot(a_ss, vh[second], preferred_element_type=F32))

        b_end = bch[end:end + 1, :]
        kdec = (kh * jnp.exp(b_end - bch)).astype(BF16)
        carry = _col_rep(jnp.exp(b_end))
        s_ref[direction, h] = _wide(carry) * s_old + _dot_tn(kdec, vh)

    return head


def _scan_kernel(*refs, nc, emit_state):
    per_dir = [refs[0:7], refs[7:14]]
    gb_ref, wgk_ref, bgk_ref, c0_ref, n0_ref, m0_ref, s0_ref = refs[14:21]
    hm_refs = refs[21:23]
    hg_refs = refs[23:25]
    rest = refs[25:]
    if emit_state:
        cf_ref, nf_ref, mf_ref, sf_ref = rest[:4]
        rest = rest[4:]
    c_ref, n_ref, m_ref, s_ref = rest
    c = pl.program_id(1)

    @pl.when(c == 0)
    def _():
        c_ref[...] = c0_ref[...]
        n_ref[...] = n0_ref[...]
        m_ref[...] = m0_ref[...]
        s_ref[...] = s0_ref[...]

    chains = []
    for direction in range(2):
        mq, mk, mv, small, gq, gk, gv = per_dir[direction]
        chains.append(_mlstm_chain(mq, mk, mv, small, gb_ref, hm_refs[direction], c_ref, n_ref, m_ref,
                                   direction=direction))
        chains.append(_gla_chain(gq, gk, gv, small, wgk_ref, bgk_ref, hg_refs[direction], s_ref,
                                 direction=direction))
    for h in range(HEADS):
        for chain_head in chains:
            chain_head(h)

    if emit_state:
        @pl.when(c == nc - 1)
        def _():
            cf_ref[...] = c_ref[...]
            nf_ref[...] = n_ref[...]
            mf_ref[...] = m_ref[...]
            sf_ref[...] = s_ref[...]


def _mixer_scan(q, k, proj, gbias, wgk, bgk, c0, n0, m0, s0, *, batch, seq, emit_state):
    L = SCAN_L
    nc = seq // L
    m = batch * seq
    qk_w, v_w = HEADS * DQK, HEADS * DV

    def dir_specs(rev):
        def row(b, c):
            return b * nc + (nc - 1 - c if rev else c)
        spec = lambda w, col: pl.BlockSpec((L, w), lambda b, c: (row(b, c), col))
        ins = [spec(qk_w, 0), spec(qk_w, 0), spec(v_w, COL_AV // v_w), spec(128, COL_SMALL // 128),
               spec(qk_w, COL_BQ // qk_w), spec(qk_w, COL_BK // qk_w), spec(v_w, COL_BV // v_w)]
        return ins, spec(v_w, 0)

    ins_f, out_f = dir_specs(False)
    ins_r, out_r = dir_specs(True)
    state = lambda *tail: pl.BlockSpec((None, 2, HEADS) + tail, lambda b, c: (b, 0, 0) + (0,) * len(tail))
    in_specs = ins_f + ins_r + [
        pl.BlockSpec((1, 128), lambda b, c: (0, 0)),
        pl.BlockSpec((2, 128, qk_w), lambda b, c: (0, 0, 0)),
        pl.BlockSpec((2, 1, qk_w), lambda b, c: (0, 0, 0)),
        state(DQK, DV), state(DQK), state(128), state(DQK, DV),
    ]
    out_specs = [out_f, out_r, out_f, out_r]
    out_shape = [jax.ShapeDtypeStruct((m, v_w), F32)] * 4
    state_shapes = [(batch, 2, HEADS, DQK, DV), (batch, 2, HEADS, DQK), (batch, 2, HEADS, 128),
                    (batch, 2, HEADS, DQK, DV)]
    if emit_state:
        out_specs += [state(DQK, DV), state(DQK), state(128), state(DQK, DV)]
        out_shape += [jax.ShapeDtypeStruct(sh, F32) for sh in state_shapes]
    per_dir_args = [q, k, proj, proj, proj, proj, proj]
    return pl.pallas_call(
        functools.partial(_scan_kernel, nc=nc, emit_state=emit_state),
        grid=(batch, nc),
        in_specs=in_specs,
        out_specs=out_specs,
        out_shape=out_shape,
        scratch_shapes=[pltpu.VMEM(sh[1:], F32) for sh in state_shapes],
        compiler_params=_cparams(("parallel", "arbitrary")),
        name="mixer_scan",
    )(*per_dir_args, *per_dir_args, gbias, wgk, bgk, c0, n0, m0, s0)


def _head_rms(x):
    parts = []
    for h in range(HEADS):
        xh = x[:, h * DV:(h + 1) * DV]
        ms = jnp.mean(xh * xh, axis=-1, keepdims=True)
        parts.append(xh * lax.rsqrt(ms + RMS_EPS))
    return jnp.concatenate(parts, axis=1)


def _mixout_kernel(x_ref, mod_ref, hmf_ref, hmr_ref, hgf_ref, hgr_ref, ao_ref, bg_ref,
                   ag_ref, bgn_ref, w_ref, o_ref):
    ya = _head_rms(hmf_ref[...] + hmr_ref[...]) * ag_ref[...] * jax.nn.sigmoid(ao_ref[...])
    bg = bg_ref[...]
    yb = _head_rms(hgf_ref[...] + hgr_ref[...]) * bgn_ref[...] * (bg * jax.nn.sigmoid(bg))
    hw = HEADS * DV
    y = jnp.dot(ya.astype(BF16), w_ref[0:hw, :], preferred_element_type=F32)
    y = y + jnp.dot(yb.astype(BF16), w_ref[hw:2 * hw, :], preferred_element_type=F32)
    o_ref[...] = x_ref[...] + mod_ref[...] * y


def _mixer_out(x, gate, hmf, hmr, hgf, hgr, proj, a_gain, b_gain, w_out):
    m, d = x.shape
    tm = OUT_TM
    hw = HEADS * DV
    wide = lambda col: pl.BlockSpec((tm, hw), lambda i: (i, col))
    return pl.pallas_call(
        _mixout_kernel,
        grid=(m // tm,),
        in_specs=[
            pl.BlockSpec((tm, d), lambda i: (i, 0)),
            pl.BlockSpec((None, 1, d), _mod_index(tm)),
            wide(0), wide(0), wide(0), wide(0),
            wide(COL_AO // hw), wide(COL_BG // hw),
            pl.BlockSpec((1, hw), lambda i: (0, 0)),
            pl.BlockSpec((1, hw), lambda i: (0, 0)),
            pl.BlockSpec((2 * hw, d), lambda i: (0, 0)),
        ],
        out_specs=pl.BlockSpec((tm, d), lambda i: (i, 0)),
        out_shape=jax.ShapeDtypeStruct((m, d), F32),
        compiler_params=_cparams(("parallel",)),
        name="mixer_out",
    )(x, gate, hmf, hmr, hgf, hgr, proj, proj, a_gain, b_gain, w_out)


def _rope(x, cos, sin):
    n = x.shape[1]
    lane = lax.broadcasted_iota(jnp.int32, x.shape, 1)
    partner = jnp.where((lane % 64) < 32, pltpu.roll(x, n - 32, 1), pltpu.roll(x, 32, 1))
    reps = n // HEAD_DIM
    cos_t = jnp.concatenate([cos] * reps, axis=1) if reps > 1 else cos
    sin_t = jnp.concatenate([sin] * reps, axis=1) if reps > 1 else sin
    return x * cos_t + partner * sin_t


def _head_norm(x, g, n_heads):
    parts = []
    for h in range(n_heads):
        xh = x[:, h * HEAD_DIM:(h + 1) * HEAD_DIM]
        ms = jnp.mean(xh * xh, axis=-1, keepdims=True)
        parts.append(xh * lax.rsqrt(ms + RMS_EPS) * g)
    return jnp.concatenate(parts, axis=1)


def _qkprep_kernel(q_in, k_in, v_in, qg_ref, kg_ref, *rest, rope):
    if rope:
        cos_ref, sin_ref, q_ref, k_ref, v_ref = rest
    else:
        q_ref, k_ref, v_ref, kf_ref = rest
    q = _head_norm(q_in[...], qg_ref[...], N_Q_HEADS)
    k = _head_norm(k_in[...], kg_ref[...], HEADS)
    if rope:
        q = _rope(q, cos_ref[...], sin_ref[...])
        k = _rope(k, cos_ref[...], sin_ref[...])
    else:
        kf_ref[...] = k
    q_ref[...] = (q * (HEAD_DIM ** -0.5 * LOG2E)).astype(BF16)
    k_ref[...] = k.astype(BF16)
    v_ref[...] = v_in[...].astype(BF16)


def _qk_prep(proj, qg, kg, rope_tabs):
    m = proj.shape[0]
    tr = PREP_TR
    nq = N_Q_HEADS * HEAD_DIM
    nk = HEADS * HEAD_DIM
    rope = rope_tabs is not None
    in_specs = [
        pl.BlockSpec((tr, nq), lambda i: (i, 0)),
        pl.BlockSpec((tr, nk), lambda i: (i, nq // nk)),
        pl.BlockSpec((tr, nk), lambda i: (i, nq // nk + 1)),
        pl.BlockSpec((1, HEAD_DIM), lambda i: (0, 0)),
        pl.BlockSpec((1, HEAD_DIM), lambda i: (0, 0)),
    ]
    args = [proj, proj, proj, qg, kg]
    out_specs = [
        pl.BlockSpec((tr, nq), lambda i: (i, 0)),
        pl.BlockSpec((tr, nk), lambda i: (i, 0)),
        pl.BlockSpec((tr, nk), lambda i: (i, 0)),
    ]
    out_shape = [
        jax.ShapeDtypeStruct((m, nq), BF16),
        jax.ShapeDtypeStruct((m, nk), BF16),
        jax.ShapeDtypeStruct((m, nk), BF16),
    ]
    if rope:
        nt = rope_tabs[0].shape[0] // tr
        in_specs += [pl.BlockSpec((tr, HEAD_DIM), lambda i: (i % nt, 0))] * 2
        args += list(rope_tabs)
    else:
        out_specs.append(pl.BlockSpec((tr, nk), lambda i: (i, 0)))
        out_shape.append(jax.ShapeDtypeStruct((m, nk), F32))
    return pl.pallas_call(
        functools.partial(_qkprep_kernel, rope=rope),
        grid=(m // tr,),
        in_specs=in_specs,
        out_specs=out_specs,
        out_shape=out_shape,
        compiler_params=_cparams(("parallel",)),
        name="qk_prep",
    )(*args)


def _attn_kernel(q_ref, k_ref, v_ref, o_ref, sc_ref, *, tq, sk):
    kc = min(ATT_KC, sk)
    chunks = [slice(j * kc, (j + 1) * kc) for j in range(sk // kc)]
    n_chains = ATT_KVG * Q_PER_KV
    head = lambda j: slice(j * HEAD_DIM, (j + 1) * HEAD_DIM)
    kv = lambda j: head(j // Q_PER_KV)

    row_max = [None] * n_chains

    def scores_pass(j):
        qj = q_ref[:, head(j)]
        mrun = None
        for ch in chunks:
            s = _dot_nt(qj, k_ref[ch, kv(j)])
            sc_ref[j, :, ch] = s
            for b in range(kc // 128):
                sb = s[:, b * 128:(b + 1) * 128]
                mrun = sb if mrun is None else jnp.maximum(mrun, sb)
            yield
        row_max[j] = jnp.max(mrun, axis=-1, keepdims=True)

    def values_pass(j):
        m = row_max[j]
        lrun = jnp.zeros((tq, 128), F32)
        acc = jnp.zeros((tq, HEAD_DIM), F32)
        for ch in chunks:
            p = jnp.exp2(sc_ref[j, :, ch] - m)
            for b in range(kc // 128):
                lrun = lrun + p[:, b * 128:(b + 1) * 128]
            acc = acc + jnp.dot(p.astype(BF16), v_ref[ch, kv(j)], preferred_element_type=F32)
            yield
        o = acc * (1.0 / jnp.sum(lrun, axis=-1, keepdims=True))
        o_ref[:, head(j)] = o.astype(BF16)

    order = [scores_pass(0)]
    for j in range(1, n_chains):
        order += [scores_pass(j), values_pass(j - 1)]
    order.append(values_pass(n_chains - 1))
    for task in order:
        for _ in task:
            pass


def _attention(q, k, v, *, batch, sq, sk):
    tq = ATT_TQ
    nq = sq // tq
    gw = ATT_KVG * Q_PER_KV * HEAD_DIM
    kw = ATT_KVG * HEAD_DIM
    return pl.pallas_call(
        functools.partial(_attn_kernel, tq=tq, sk=sk),
        scratch_shapes=[pltpu.VMEM((ATT_KVG * Q_PER_KV, tq, sk), F32)],
        grid=(batch, HEADS // ATT_KVG, nq),
        in_specs=[
            pl.BlockSpec((tq, gw), lambda b, g, i: (b * nq + i, g)),
            pl.BlockSpec((sk, kw), lambda b, g, i: (b, g)),
            pl.BlockSpec((sk, kw), lambda b, g, i: (b, g)),
        ],
        out_specs=pl.BlockSpec((tq, gw), lambda b, g, i: (b * nq + i, g)),
        out_shape=jax.ShapeDtypeStruct((batch * sq, N_Q_HEADS * HEAD_DIM), BF16),
        compiler_params=_cparams(("parallel", "parallel", "arbitrary")),
        name="gqa_attention",
    )(q, k, v)


def _oproj_kernel(x_ref, mod_ref, a_ref, w_ref, o_ref):
    y = jnp.dot(a_ref[...], w_ref[...], preferred_element_type=F32)
    o_ref[...] = x_ref[...] + mod_ref[...] * y


def _attn_out(x, gate, a, w_o):
    m, d = x.shape
    tm = ATTN_OUT_TM
    return pl.pallas_call(
        _oproj_kernel,
        grid=(m // tm,),
        in_specs=[
            pl.BlockSpec((tm, d), lambda i: (i, 0)),
            pl.BlockSpec((None, 1, d), _mod_index(tm)),
            pl.BlockSpec((tm, a.shape[1]), lambda i: (i, 0)),
            pl.BlockSpec(w_o.shape, lambda i: (0, 0)),
        ],
        out_specs=pl.BlockSpec((tm, d), lambda i: (i, 0)),
        out_shape=jax.ShapeDtypeStruct((m, d), F32),
        compiler_params=_cparams(("parallel",)),
        name="attn_out",
    )(x, gate, a, w_o)


def _permute_w_in(w):
    w = w.astype(BF16)
    pad = jnp.zeros((w.shape[0], AB_PAD_COLS - 6192), BF16)
    return jnp.concatenate([w[:, :3072], w[:, 3088:6160], w[:, 3072:3088], w[:, 6160:6192], pad], axis=1)


def _rope_tables(n_tokens):
    t = jnp.arange(n_tokens)
    row = (t // GRID_W).astype(F32)
    col = (t % GRID_W).astype(F32)
    inv = ROPE_THETA ** (-jnp.arange(0, ROPE_AXIS, 2, dtype=F32) / ROPE_AXIS)
    ar = row[:, None] * inv
    ac = col[:, None] * inv
    cos = jnp.concatenate([jnp.cos(ar), jnp.cos(ar), jnp.cos(ac), jnp.cos(ac)], axis=1)
    sin = jnp.concatenate([-jnp.sin(ar), jnp.sin(ar), -jnp.sin(ac), jnp.sin(ac)], axis=1)
    return cos, sin


def kernel(x_prompt, x_sample, c, c_ctx, state_mlstm_C, state_mlstm_n, state_mlstm_m, state_gla_S,
           cache_k, cache_v, w_mod, b_mod, norm_g, ffn_w_gate, ffn_w_up, ffn_w_down, w_in_ab,
           mlstm_conv_w, mlstm_conv_b, mlstm_b_i, mlstm_b_f, mlstm_out_g, gla_w_gk, gla_b_gk, gla_out_g,
           w_out_ab, w_qkv, q_norm_g, k_norm_g, w_o):
    bp, sp, d = x_prompt.shape
    bs, ss, _ = x_sample.shape
    depth = w_mod.shape[0]
    xp = x_prompt.reshape(bp * sp, d)
    xs = x_sample.reshape(bs * ss, d)

    c8 = jnp.concatenate([c_ctx[None, :], c, jnp.zeros((8 - 1 - bs, d), F32)], axis=0)
    mod = _modulation(c8, w_mod, b_mod).reshape(depth, 8, N_MOD, d)

    ffn_w = (ffn_w_gate.astype(BF16), ffn_w_up.astype(BF16), ffn_w_down.astype(BF16))
    new_c, new_n, new_m, new_s, new_k, new_v = [], [], [], [], [], []
    for l in range(depth):
        mod_p = mod[l, 0:1]
        mod_s = mod[l, 1:1 + bs]

        def ffn_half(x, md, j, half):
            return _ffn(x, md[:, 3 * j:3 * j + 3], norm_g[l, j][None, :], *ffn_w, l, half)

        xp = ffn_half(xp, mod_p, 0, 0)
        xs = ffn_half(xs, mod_s, 0, 0)
        g_mix = norm_g[l, 1][None, :]
        if l % 2 == 0:
            e = l // 2
            w_in = _permute_w_in(w_in_ab[e])
            w_out = w_out_ab[e].astype(BF16)
            gbias = jnp.zeros((2, 2, HEADS), F32)
            gbias = gbias.at[:, 0].set(mlstm_b_i[e]).at[:, 1].set(mlstm_b_f[e])
            gbias = jnp.pad(gbias.reshape(1, 16), ((0, 0), (0, 128 - 16)))
            wgk = jnp.zeros((2, 128, HEADS * DQK), F32)
            for j in range(2):
                lo = LR_OFF + j * GLA_RANK
                wgk = wgk.at[j, lo:lo + GLA_RANK].set(gla_w_gk[e, j])
            wgk = wgk.astype(BF16)
            bgk = gla_b_gk[e][:, None, :]
            conv_w = mlstm_conv_w[e]
            conv_b = mlstm_conv_b[e][None, :]
            a_gain = mlstm_out_g[e][None, :]
            b_gain = gla_out_g[e][None, :]

            def mixer(x, md, batch, seq, c0, n0, m0, s0, emit_state):
                proj = _proj(x, md[:, 3:6], g_mix, w_in, 1280)
                q, k = _conv_qk(proj, conv_w, conv_b, seq)
                m0r = jnp.broadcast_to(m0[..., None], m0.shape + (128,))
                r = _mixer_scan(q, k, proj, gbias, wgk, bgk, c0, n0, m0r, s0,
                                batch=batch, seq=seq, emit_state=emit_state)
                y = _mixer_out(x, md[:, 5:6], r[0], r[1], r[2], r[3], proj, a_gain, b_gain, w_out)
                return y, r[4:]

            zc = jnp.zeros((bp, 2, HEADS, DQK, DV), F32)
            zn = jnp.zeros((bp, 2, HEADS, DQK), F32)
            zm = jnp.zeros((bp, 2, HEADS), F32)
            xp, st = mixer(xp, mod_p, bp, sp, zc, zn, zm, zc, True)
            xs, _ = mixer(xs, mod_s, bs, ss, state_mlstm_C[:, e], state_mlstm_n[:, e],
                          state_mlstm_m[:, e], state_gla_S[:, e], False)
            new_c.append(st[0])
            new_n.append(st[1])
            new_m.append(st[2][..., 0])
            new_s.append(st[3])
        else:
            o = l // 2
            wq = w_qkv[o].astype(BF16)
            wo = w_o[o].astype(BF16)
            qg = q_norm_g[o][None, :]
            kg = k_norm_g[o][None, :]
            nk = HEADS * HEAD_DIM
            proj_p = _proj(xp, mod_p[:, 3:6], g_mix, wq, 1024)
            q, k, v, kf = _qk_prep(proj_p, qg, kg, None)
            a = _attention(q, k, v, batch=bp, sq=sp, sk=sp)
            xp = _attn_out(xp, mod_p[:, 5:6], a, wo)
            new_k.append(kf.reshape(bp, sp, HEADS, HEAD_DIM))
            new_v.append(proj_p[:, N_Q_HEADS * HEAD_DIM + nk:].reshape(bp, sp, HEADS, HEAD_DIM))
            proj_s = _proj(xs, mod_s[:, 3:6], g_mix, wq, 1024)
            q, k, v = _qk_prep(proj_s, qg, kg, _rope_tables(ss))
            past = cache_k.shape[2]
            ck = cache_k[:, o].reshape(bs, past, nk).astype(BF16)
            cv = cache_v[:, o].reshape(bs, past, nk).astype(BF16)
            k_all = jnp.concatenate([ck, k.reshape(bs, ss, nk)], axis=1).reshape(bs * (past + ss), nk)
            v_all = jnp.concatenate([cv, v.reshape(bs, ss, nk)], axis=1).reshape(bs * (past + ss), nk)
            a = _attention(q, k_all, v_all, batch=bs, sq=ss, sk=past + ss)
            xs = _attn_out(xs, mod_s[:, 5:6], a, wo)
        xp = ffn_half(xp, mod_p, 2, 1)
        xs = ffn_half(xs, mod_s, 2, 1)

    return (xp.reshape(bp, sp, d), xs.reshape(bs, ss, d),
            jnp.stack(new_c, axis=1), jnp.stack(new_n, axis=1), jnp.stack(new_m, axis=1),
            jnp.stack(new_s, axis=1), jnp.stack(new_k, axis=1), jnp.stack(new_v, axis=1))
```

```python
import functools

import jax
import jax.numpy as jnp
from jax import lax
from jax.experimental import pallas as pl
from jax.experimental.pallas import tpu as pltpu

F32 = jnp.float32
BF16 = jnp.bfloat16

D_MODEL = 2048
N_MOD = 9
D_FF = 5632
RMS_EPS = 1e-6
ROWS_PER_MOD = 4096
HEADS = 4
DQK = 128
DV = 256
GLA_RANK = 16
GLA_GATE_NORM = 16.0
CONV_W = 3
N_Q_HEADS = 16
Q_PER_KV = 4
HEAD_DIM = 128
GRID_W = 64
ROPE_AXIS = HEAD_DIM // 2
ROPE_THETA = 10000.0

COL_AQ, COL_AK, COL_AV, COL_AO = 0, 512, 1024, 2048
COL_BQ, COL_BK, COL_BV, COL_BG = 3072, 3584, 4096, 5120
COL_SMALL = 6144
AB_PAD_COLS = 6400
LR_OFF = 16

VMEM_LIMIT = 60 * 1024 * 1024
SCAN_L = 128
FFN_TM, FFN_TF = 1024, 512
PROJ_TM = 1024
ATTN_OUT_TM = 512
OUT_TM = 256
PREP_TR = 256
QKPREP_TR = 512
ATT_TQ = 128
ATT_KVG = 2
ATT_KC = 256
LOG2E = 1.4426950408889634


def _cparams(sem):
    return pltpu.CompilerParams(dimension_semantics=sem, vmem_limit_bytes=VMEM_LIMIT)


def _log_sigmoid(x):
    return jnp.minimum(x, 0.0) - jnp.log1p(jnp.exp(-jnp.abs(x)))


def _adaln(x, g, shift, scale):
    ms = jnp.mean(x * x, axis=-1, keepdims=True)
    return (x * lax.rsqrt(ms + RMS_EPS) * g) * (1.0 + scale) + shift


def _mod_index(tm):
    return lambda i, *_: ((i * tm) // ROWS_PER_MOD, 0, 0)


def _mod_kernel(c_ref, w_ref, b_ref, o_ref):
    c = c_ref[...]
    s = (c * jax.nn.sigmoid(c)).astype(BF16)
    o_ref[...] = jnp.dot(s, w_ref[...].astype(BF16), preferred_element_type=F32) + b_ref[...]


def _modulation(c8, w_mod, b_mod):
    depth, d, n = w_mod.shape
    tn = 1024
    return pl.pallas_call(
        _mod_kernel,
        grid=(depth, n // tn),
        in_specs=[
            pl.BlockSpec((8, d), lambda l, j: (0, 0)),
            pl.BlockSpec((None, d, tn), lambda l, j: (l, 0, j)),
            pl.BlockSpec((None, 1, tn), lambda l, j: (l, 0, j)),
        ],
        out_specs=pl.BlockSpec((None, 8, tn), lambda l, j: (l, 0, j)),
        out_shape=jax.ShapeDtypeStruct((depth, 8, n), F32),
        compiler_params=_cparams(("arbitrary", "arbitrary")),
        name="modulation",
    )(c8, w_mod, b_mod.reshape(depth, 1, n))


def _ffn_kernel(x_ref, mod_ref, g_ref, wg_hbm, wu_hbm, wd_hbm, o_ref, h_ref, wg_buf, wu_buf, wd_buf, sem,
                *, nf, tf, layer, half):
    def copies(f):
        slot = f % 2
        cols = pl.ds(f * tf, tf)
        return (
            pltpu.make_async_copy(wg_hbm.at[layer, half, :, cols], wg_buf.at[slot], sem.at[0, slot]),
            pltpu.make_async_copy(wu_hbm.at[layer, half, :, cols], wu_buf.at[slot], sem.at[1, slot]),
            pltpu.make_async_copy(wd_hbm.at[layer, half, cols, :], wd_buf.at[slot], sem.at[2, slot]),
        )

    def start(f):
        for cp in copies(f):
            cp.start()

    def wait(f):
        for cp in copies(f):
            cp.wait()

    i = pl.program_id(0)
    not_last_row_tile = i < pl.num_programs(0) - 1

    @pl.when(i == 0)
    def _():
        start(0)
        start(1)

    wait(0)
    h_ref[...] = _adaln(x_ref[...], g_ref[...], mod_ref[0:1, :], mod_ref[1:2, :]).astype(BF16)
    for f in range(nf):
        slot = f % 2
        if f > 0:
            wait(f)
        h = h_ref[...]
        gate = jnp.dot(h, wg_buf[slot], preferred_element_type=F32)
        up = jnp.dot(h, wu_buf[slot], preferred_element_type=F32)
        a = (gate * jax.nn.sigmoid(gate) * up).astype(BF16)
        part = jnp.dot(a, wd_buf[slot], preferred_element_type=F32)
        if f + 2 < nf:
            start(f + 2)
        else:
            pl.when(not_last_row_tile)(functools.partial(start, slot))
        if f == 0:
            o_ref[...] = part
        elif f < nf - 1:
            o_ref[...] += part
        else:
            o_ref[...] = x_ref[...] + (0.5 * mod_ref[2:3, :]) * (o_ref[...] + part)


def _ffn(x, mod3, g, wg, wu, wd, layer, half):
    m, d = x.shape
    tm, tf = FFN_TM, FFN_TF
    nf = D_FF // tf
    return pl.pallas_call(
        functools.partial(_ffn_kernel, nf=nf, tf=tf, layer=layer, half=half),
        grid=(m // tm,),
        in_specs=[
            pl.BlockSpec((tm, d), lambda i: (i, 0)),
            pl.BlockSpec((None, 3, d), _mod_index(tm)),
            pl.BlockSpec((1, d), lambda i: (0, 0)),
            pl.BlockSpec(memory_space=pl.ANY),
            pl.BlockSpec(memory_space=pl.ANY),
            pl.BlockSpec(memory_space=pl.ANY),
        ],
        out_specs=pl.BlockSpec((tm, d), lambda i: (i, 0)),
        out_shape=jax.ShapeDtypeStruct((m, d), F32),
        scratch_shapes=[
            pltpu.VMEM((tm, d), BF16),
            pltpu.VMEM((2, d, tf), BF16),
            pltpu.VMEM((2, d, tf), BF16),
            pltpu.VMEM((2, tf, d), BF16),
            pltpu.SemaphoreType.DMA((3, 2)),
        ],
        compiler_params=_cparams(("arbitrary",)),
        name="ffn",
    )(x, mod3, g, wg, wu, wd)


def _proj_kernel(x_ref, mod_ref, g_ref, w_ref, o_ref, h_ref):
    j = pl.program_id(1)

    @pl.when(j == 0)
    def _():
        h = _adaln(x_ref[...], g_ref[...], mod_ref[0:1, :], mod_ref[1:2, :]).astype(BF16)
        h_ref[...] = h
        o_ref[...] = jnp.dot(h, w_ref[...], preferred_element_type=F32)

    @pl.when(j > 0)
    def _():
        o_ref[...] = jnp.dot(h_ref[...], w_ref[...], preferred_element_type=F32)


def _proj(x, mod3, g, w, tn):
    m, d = x.shape
    n = w.shape[1]
    tm = PROJ_TM
    return pl.pallas_call(
        _proj_kernel,
        grid=(m // tm, n // tn),
        in_specs=[
            pl.BlockSpec((tm, d), lambda i, j: (i, 0)),
            pl.BlockSpec((None, 3, d), _mod_index(tm)),
            pl.BlockSpec((1, d), lambda i, j: (0, 0)),
            pl.BlockSpec((d, tn), lambda i, j: (0, j)),
        ],
        out_specs=pl.BlockSpec((tm, tn), lambda i, j: (i, j)),
        out_shape=jax.ShapeDtypeStruct((m, n), F32),
        scratch_shapes=[pltpu.VMEM((tm, d), BF16)],
        compiler_params=_cparams(("parallel", "arbitrary")),
        name="adaln_proj",
    )(x, mod3, g, w)


def _conv_kernel(x_ref, prev_ref, next_ref, w_ref, b_ref, q_ref, k_ref, *, tr, seq):
    i = pl.program_id(0)
    x = x_ref[...]
    first = (i * tr) % seq == 0
    last = ((i + 1) * tr) % seq == 0
    prev_row = jnp.where(first, 0.0, prev_ref[7:8, :])
    next_row = jnp.where(last, 0.0, next_ref[0:1, :])
    ridx = lax.broadcasted_iota(jnp.int32, x.shape, 0)
    xm = jnp.where(ridx == 0, prev_row, pltpu.roll(x, 1, 0))
    xp = jnp.where(ridx == tr - 1, next_row, pltpu.roll(x, tr - 1, 0))
    y = xm * w_ref[0:1, :] + x * w_ref[1:2, :] + xp * w_ref[2:3, :] + b_ref[...]
    y = y * jax.nn.sigmoid(y)
    hqk = HEADS * DQK
    q_ref[...] = y[:, :hqk].astype(BF16)
    k_ref[...] = (y[:, hqk:] * (DQK ** -0.5)).astype(BF16)


def _conv_qk(proj, conv_w, conv_b, seq):
    m = proj.shape[0]
    tr = PREP_TR
    c = 2 * HEADS * DQK
    nb8 = m // 8
    return pl.pallas_call(
        functools.partial(_conv_kernel, tr=tr, seq=seq),
        grid=(m // tr,),
        in_specs=[
            pl.BlockSpec((tr, c), lambda i: (i, 0)),
            pl.BlockSpec((8, c), lambda i: (jnp.maximum(i * (tr // 8) - 1, 0), 0)),
            pl.BlockSpec((8, c), lambda i: (jnp.minimum((i + 1) * (tr // 8), nb8 - 1), 0)),
            pl.BlockSpec((CONV_W, c), lambda i: (0, 0)),
            pl.BlockSpec((1, c), lambda i: (0, 0)),
        ],
        out_specs=[
            pl.BlockSpec((tr, c // 2), lambda i: (i, 0)),
            pl.BlockSpec((tr, c // 2), lambda i: (i, 0)),
        ],
        out_shape=[jax.ShapeDtypeStruct((m, c // 2), BF16)] * 2,
        compiler_params=_cparams(("parallel",)),
        name="mlstm_conv",
    )(proj, proj, proj, conv_w, conv_b)


def _order_mask(n, rev):
    t = lax.broadcasted_iota(jnp.int32, (n, n), 0)
    s = lax.broadcasted_iota(jnp.int32, (n, n), 1)
    return (s >= t) if rev else (s <= t)


def _masked_sum(mask, x):
    n = x.shape[1]
    hi = x.astype(BF16)
    r = x - hi.astype(F32)
    mid = r.astype(BF16)
    lo = (r - mid.astype(F32)).astype(BF16)
    y = jnp.dot(mask.astype(BF16), jnp.concatenate([hi, mid, lo], axis=1), preferred_element_type=F32)
    return y[:, :n] + y[:, n:2 * n] + y[:, 2 * n:]


def _col_rep(row):
    return jnp.broadcast_to(row, (128, 128)).T


def _dot_nt(a, b):
    return lax.dot_general(a, b, (((1,), (1,)), ((), ())), preferred_element_type=F32)


def _dot_tn(a, b):
    return lax.dot_general(a, b, (((0,), (0,)), ((), ())), preferred_element_type=F32)


def _wide(x):
    return jnp.concatenate([x, x], axis=1)


def _mlstm_chain(q_ref, k_ref, v_ref, g_ref, gb_ref, h_ref, c_ref, n_ref, m_ref, *, direction):
    L = SCAN_L
    rev = direction == 1
    mask = _order_mask(L, rev)
    gates = g_ref[...] + gb_ref[...]
    logf = _log_sigmoid(gates)
    bsum = _masked_sum(mask, logf)
    col_i = [8 * direction + h for h in range(HEADS)]
    col_f = [8 * direction + HEADS + h for h in range(HEADS)]
    gates_t = gates.T
    bsum_t = bsum.T
    end = 0 if rev else L - 1

    def head(h):
        hs = slice(h * DQK, (h + 1) * DQK)
        vs = slice(h * DV, (h + 1) * DV)
        b_r = bsum_t[col_f[h]:col_f[h] + 1, :]
        i_r = gates_t[col_i[h]:col_i[h] + 1, :]
        b_c = jnp.broadcast_to(bsum[:, col_f[h]:col_f[h] + 1], (L, 128))
        i_c = jnp.broadcast_to(gates[:, col_i[h]:col_i[h] + 1], (L, 128))
        m_prev = m_ref[direction, h:h + 1, :]
        n_prev = n_ref[direction, h:h + 1, :]
        dmat = jnp.where(mask, b_c - b_r + i_r, -jnp.inf)
        inter = b_c + m_prev
        m_t = jnp.maximum(inter, jnp.max(dmat, axis=-1, keepdims=True))
        w_intra = jnp.exp(dmat - m_t)
        w_inter = jnp.exp(inter - m_t)
        qh = q_ref[:, hs]
        kh = k_ref[:, hs]
        vh = v_ref[:, vs].astype(BF16)
        s = _dot_nt(qh, kh) * w_intra
        c_old = c_ref[direction, h]
        num = _wide(w_inter) * jnp.dot(qh, c_old.astype(BF16), preferred_element_type=F32)
        num = num + jnp.dot(s.astype(BF16), vh, preferred_element_type=F32)
        qn = jnp.sum(qh.astype(F32) * n_prev, axis=-1, keepdims=True)
        den = w_inter * qn + jnp.sum(s, axis=-1, keepdims=True)
        inv = 1.0 / jnp.maximum(jnp.abs(den), jnp.exp(-m_t))
        h_ref[:, vs] = num * _wide(inv)

        g_end = b_c[end:end + 1, :]
        dec = g_end - b_c + i_c
        m_new = jnp.maximum(g_end + m_prev, jnp.max(dec, axis=0, keepdims=True))
        ws = jnp.exp(dec - m_new)
        wc = jnp.exp(g_end + m_prev - m_new)
        kw = kh.astype(F32) * ws
        c_ref[direction, h] = _wide(wc) * c_old + _dot_tn(kw.astype(BF16), vh)
        n_ref[direction, h:h + 1, :] = wc * n_prev + jnp.sum(kw, axis=0, keepdims=True)
        m_ref[direction, h:h + 1, :] = m_new

    return head


def _gla_chain(q_ref, k_ref, v_ref, g_ref, wgk_ref, bgk_ref, o_ref, s_ref, *, direction):
    L = SCAN_L
    H = L // 2
    rev = direction == 1
    glogit = jnp.dot(g_ref[...].astype(BF16), wgk_ref[direction], preferred_element_type=F32)
    la = _log_sigmoid(glogit + bgk_ref[direction]) * (1.0 / GLA_GATE_NORM)
    bc = _masked_sum(_order_mask(L, rev), la)
    mask_h = _order_mask(H, rev)
    first = slice(H, L) if rev else slice(0, H)
    second = slice(0, H) if rev else slice(H, L)
    first_end = H if rev else H - 1
    end = 0 if rev else L - 1
    scale = DQK ** -0.5

    def scores(qx, kx, bq, bk, anchor):
        qt = (qx * jnp.exp(bq - anchor)).astype(BF16)
        kt = (kx * jnp.exp(anchor - bk)).astype(BF16)
        return _dot_nt(qt, kt)

    def head(h):
        hs = slice(h * DQK, (h + 1) * DQK)
        vs = slice(h * DV, (h + 1) * DV)
        bch = bc[:, hs]
        qh = q_ref[:, hs] * scale
        kh = k_ref[:, hs]
        vh = v_ref[:, vs].astype(BF16)
        s_old = s_ref[direction, h]
        o = jnp.dot((qh * jnp.exp(bch)).astype(BF16), s_old.astype(BF16), preferred_element_type=F32)

        def diag(rows):
            mid = rows.start + H // 2
            a = scores(qh[rows], kh[rows], bch[rows], bch[rows], bch[mid:mid + 1, :])
            return jnp.where(mask_h, a, 0.0).astype(BF16)

        a_ff = diag(first)
        a_ss = diag(second)
        a_sf = scores(qh[second], kh[first], bch[second], bch[first],
                      bch[first_end:first_end + 1, :]).astype(BF16)
        o_ref[first, vs] = o[first] + jnp.dot(a_ff, vh[first], preferred_element_type=F32)
        o_ref[second, vs] = (o[second] + jnp.dot(a_sf, vh[first], preferred_element_type=F32)
                             + jnp.dot(a_ss, vh[second], preferred_element_type=F32))

        b_end = bch[end:end + 1, :]
        kdec = (kh * jnp.exp(b_end - bch)).astype(BF16)
        carry = _col_rep(jnp.exp(b_end))
        s_ref[direction, h] = _wide(carry) * s_old + _dot_tn(kdec, vh)

    return head


def _scan_kernel(*refs, nc, emit_state):
    per_dir = [refs[0:7], refs[7:14]]
    gb_ref, wgk_ref, bgk_ref, c0_ref, n0_ref, m0_ref, s0_ref = refs[14:21]
    hm_refs = refs[21:23]
    hg_refs = refs[23:25]
    rest = refs[25:]
    if emit_state:
        cf_ref, nf_ref, mf_ref, sf_ref = rest[:4]
        rest = rest[4:]
    c_ref, n_ref, m_ref, s_ref = rest
    c = pl.program_id(1)

    @pl.when(c == 0)
    def _():
        c_ref[...] = c0_ref[...]
        n_ref[...] = n0_ref[...]
        m_ref[...] = m0_ref[...]
        s_ref[...] = s0_ref[...]

    chains = []
    for direction in range(2):
        mq, mk, mv, small, gq, gk, gv = per_dir[direction]
        chains.append(_mlstm_chain(mq, mk, mv, small, gb_ref, hm_refs[direction], c_ref, n_ref, m_ref,
                                   direction=direction))
        chains.append(_gla_chain(gq, gk, gv, small, wgk_ref, bgk_ref, hg_refs[direction], s_ref,
                                 direction=direction))
    for h in range(HEADS):
        for chain_head in chains:
            chain_head(h)

    if emit_state:
        @pl.when(c == nc - 1)
        def _():
            cf_ref[...] = c_ref[...]
            nf_ref[...] = n_ref[...]
            mf_ref[...] = m_ref[...]
            sf_ref[...] = s_ref[...]


def _mixer_scan(q, k, proj, gbias, wgk, bgk, c0, n0, m0, s0, *, batch, seq, emit_state):
    L = SCAN_L
    nc = seq // L
    m = batch * seq
    qk_w, v_w = HEADS * DQK, HEADS * DV

    def dir_specs(rev):
        def row(b, c):
            return b * nc + (nc - 1 - c if rev else c)
        spec = lambda w, col: pl.BlockSpec((L, w), lambda b, c: (row(b, c), col))
        ins = [spec(qk_w, 0), spec(qk_w, 0), spec(v_w, COL_AV // v_w), spec(128, COL_SMALL // 128),
               spec(qk_w, COL_BQ // qk_w), spec(qk_w, COL_BK // qk_w), spec(v_w, COL_BV // v_w)]
        return ins, spec(v_w, 0)

    ins_f, out_f = dir_specs(False)
    ins_r, out_r = dir_specs(True)
    state = lambda *tail: pl.BlockSpec((None, 2, HEADS) + tail, lambda b, c: (b, 0, 0) + (0,) * len(tail))
    in_specs = ins_f + ins_r + [
        pl.BlockSpec((1, 128), lambda b, c: (0, 0)),
        pl.BlockSpec((2, 128, qk_w), lambda b, c: (0, 0, 0)),
        pl.BlockSpec((2, 1, qk_w), lambda b, c: (0, 0, 0)),
        state(DQK, DV), state(DQK), state(128), state(DQK, DV),
    ]
    out_specs = [out_f, out_r, out_f, out_r]
    out_shape = [jax.ShapeDtypeStruct((m, v_w), F32)] * 4
    state_shapes = [(batch, 2, HEADS, DQK, DV), (batch, 2, HEADS, DQK), (batch, 2, HEADS, 128),
                    (batch, 2, HEADS, DQK, DV)]
    if emit_state:
        out_specs += [state(DQK, DV), state(DQK), state(128), state(DQK, DV)]
        out_shape += [jax.ShapeDtypeStruct(sh, F32) for sh in state_shapes]
    per_dir_args = [q, k, proj, proj, proj, proj, proj]
    return pl.pallas_call(
        functools.partial(_scan_kernel, nc=nc, emit_state=emit_state),
        grid=(batch, nc),
        in_specs=in_specs,
        out_specs=out_specs,
        out_shape=out_shape,
        scratch_shapes=[pltpu.VMEM(sh[1:], F32) for sh in state_shapes],
        compiler_params=_cparams(("parallel", "arbitrary")),
        name="mixer_scan",
    )(*per_dir_args, *per_dir_args, gbias, wgk, bgk, c0, n0, m0, s0)


def _head_rms(x):
    parts = []
    for h in range(HEADS):
        xh = x[:, h * DV:(h + 1) * DV]
        ms = jnp.mean(xh * xh, axis=-1, keepdims=True)
        parts.append(xh * lax.rsqrt(ms + RMS_EPS))
    return jnp.concatenate(parts, axis=1)


def _mixout_kernel(x_ref, mod_ref, hmf_ref, hmr_ref, hgf_ref, hgr_ref, ao_ref, bg_ref,
                   ag_ref, bgn_ref, w_ref, o_ref):
    ya = _head_rms(hmf_ref[...] + hmr_ref[...]) * ag_ref[...] * jax.nn.sigmoid(ao_ref[...])
    bg = bg_ref[...]
    yb = _head_rms(hgf_ref[...] + hgr_ref[...]) * bgn_ref[...] * (bg * jax.nn.sigmoid(bg))
    hw = HEADS * DV
    y = jnp.dot(ya.astype(BF16), w_ref[0:hw, :], preferred_element_type=F32)
    y = y + jnp.dot(yb.astype(BF16), w_ref[hw:2 * hw, :], preferred_element_type=F32)
    o_ref[...] = x_ref[...] + mod_ref[...] * y


def _mixer_out(x, gate, hmf, hmr, hgf, hgr, proj, a_gain, b_gain, w_out):
    m, d = x.shape
    tm = OUT_TM
    hw = HEADS * DV
    wide = lambda col: pl.BlockSpec((tm, hw), lambda i: (i, col))
    return pl.pallas_call(
        _mixout_kernel,
        grid=(m // tm,),
        in_specs=[
            pl.BlockSpec((tm, d), lambda i: (i, 0)),
            pl.BlockSpec((None, 1, d), _mod_index(tm)),
            wide(0), wide(0), wide(0), wide(0),
            wide(COL_AO // hw), wide(COL_BG // hw),
            pl.BlockSpec((1, hw), lambda i: (0, 0)),
            pl.BlockSpec((1, hw), lambda i: (0, 0)),
            pl.BlockSpec((2 * hw, d), lambda i: (0, 0)),
        ],
        out_specs=pl.BlockSpec((tm, d), lambda i: (i, 0)),
        out_shape=jax.ShapeDtypeStruct((m, d), F32),
        compiler_params=_cparams(("parallel",)),
        name="mixer_out",
    )(x, gate, hmf, hmr, hgf, hgr, proj, proj, a_gain, b_gain, w_out)


def _rope(x, cos, sin):
    n = x.shape[1]
    lane = lax.broadcasted_iota(jnp.int32, x.shape, 1)
    partner = jnp.where((lane % 64) < 32, pltpu.roll(x, n - 32, 1), pltpu.roll(x, 32, 1))
    reps = n // HEAD_DIM
    cos_t = jnp.concatenate([cos] * reps, axis=1) if reps > 1 else cos
    sin_t = jnp.concatenate([sin] * reps, axis=1) if reps > 1 else sin
    return x * cos_t + partner * sin_t


def _head_norm(x, g, n_heads):
    parts = []
    for h in range(n_heads):
        xh = x[:, h * HEAD_DIM:(h + 1) * HEAD_DIM]
        ms = jnp.mean(xh * xh, axis=-1, keepdims=True)
        parts.append(xh * lax.rsqrt(ms + RMS_EPS) * g)
    return jnp.concatenate(parts, axis=1)


def _qkprep_kernel(q_in, k_in, v_in, qg_ref, kg_ref, *rest, rope):
    if rope:
        cos_ref, sin_ref, q_ref, k_ref, v_ref = rest
    else:
        q_ref, k_ref, v_ref, kf_ref = rest
    q = _head_norm(q_in[...], qg_ref[...], N_Q_HEADS)
    k = _head_norm(k_in[...], kg_ref[...], HEADS)
    if rope:
        q = _rope(q, cos_ref[...], sin_ref[...])
        k = _rope(k, cos_ref[...], sin_ref[...])
    else:
        kf_ref[...] = k
    q_ref[...] = (q * (HEAD_DIM ** -0.5 * LOG2E)).astype(BF16)
    k_ref[...] = k.astype(BF16)
    v_ref[...] = v_in[...].astype(BF16)


def _qk_prep(proj, qg, kg, rope_tabs):
    m = proj.shape[0]
    tr = QKPREP_TR
    nq = N_Q_HEADS * HEAD_DIM
    nk = HEADS * HEAD_DIM
    rope = rope_tabs is not None
    in_specs = [
        pl.BlockSpec((tr, nq), lambda i: (i, 0)),
        pl.BlockSpec((tr, nk), lambda i: (i, nq // nk)),
        pl.BlockSpec((tr, nk), lambda i: (i, nq // nk + 1)),
        pl.BlockSpec((1, HEAD_DIM), lambda i: (0, 0)),
        pl.BlockSpec((1, HEAD_DIM), lambda i: (0, 0)),
    ]
    args = [proj, proj, proj, qg, kg]
    out_specs = [
        pl.BlockSpec((tr, nq), lambda i: (i, 0)),
        pl.BlockSpec((tr, nk), lambda i: (i, 0)),
        pl.BlockSpec((tr, nk), lambda i: (i, 0)),
    ]
    out_shape = [
        jax.ShapeDtypeStruct((m, nq), BF16),
        jax.ShapeDtypeStruct((m, nk), BF16),
        jax.ShapeDtypeStruct((m, nk), BF16),
    ]
    if rope:
        nt = rope_tabs[0].shape[0] // tr
        in_specs += [pl.BlockSpec((tr, HEAD_DIM), lambda i: (i % nt, 0))] * 2
        args += list(rope_tabs)
    else:
        out_specs.append(pl.BlockSpec((tr, nk), lambda i: (i, 0)))
        out_shape.append(jax.ShapeDtypeStruct((m, nk), F32))
    return pl.pallas_call(
        functools.partial(_qkprep_kernel, rope=rope),
        grid=(m // tr,),
        in_specs=in_specs,
        out_specs=out_specs,
        out_shape=out_shape,
        compiler_params=_cparams(("parallel",)),
        name="qk_prep",
    )(*args)


def _attn_kernel(q_ref, k_ref, v_ref, o_ref, sc_ref, *, tq, sk):
    kc = min(ATT_KC, sk)
    chunks = [slice(j * kc, (j + 1) * kc) for j in range(sk // kc)]
    n_chains = ATT_KVG * Q_PER_KV
    head = lambda j: slice(j * HEAD_DIM, (j + 1) * HEAD_DIM)
    kv = lambda j: head(j // Q_PER_KV)

    row_max = [None] * n_chains

    def scores_pass(j):
        qj = q_ref[:, head(j)]
        mrun = None
        for ch in chunks:
            s = _dot_nt(qj, k_ref[ch, kv(j)])
            sc_ref[j, :, ch] = s
            for b in range(kc // 128):
                sb = s[:, b * 128:(b + 1) * 128]
                mrun = sb if mrun is None else jnp.maximum(mrun, sb)
            yield
        row_max[j] = jnp.max(mrun, axis=-1, keepdims=True)

    def values_pass(j):
        m = row_max[j]
        lrun = jnp.zeros((tq, 128), F32)
        acc = jnp.zeros((tq, HEAD_DIM), F32)
        for ch in chunks:
            p = jnp.exp2(sc_ref[j, :, ch] - m)
            for b in range(kc // 128):
                lrun = lrun + p[:, b * 128:(b + 1) * 128]
            acc = acc + jnp.dot(p.astype(BF16), v_ref[ch, kv(j)], preferred_element_type=F32)
            yield
        o = acc * (1.0 / jnp.sum(lrun, axis=-1, keepdims=True))
        o_ref[:, head(j)] = o.astype(BF16)

    order = [scores_pass(0)]
    for j in range(1, n_chains):
        order += [scores_pass(j), values_pass(j - 1)]
    order.append(values_pass(n_chains - 1))
    for task in order:
        for _ in task:
            pass


def _attention(q, k, v, *, batch, sq, sk):
    tq = ATT_TQ
    nq = sq // tq
    gw = ATT_KVG * Q_PER_KV * HEAD_DIM
    kw = ATT_KVG * HEAD_DIM
    return pl.pallas_call(
        functools.partial(_attn_kernel, tq=tq, sk=sk),
        scratch_shapes=[pltpu.VMEM((ATT_KVG * Q_PER_KV, tq, sk), F32)],
        grid=(batch, HEADS // ATT_KVG, nq),
        in_specs=[
            pl.BlockSpec((tq, gw), lambda b, g, i: (b * nq + i, g)),
            pl.BlockSpec((sk, kw), lambda b, g, i: (b, g)),
            pl.BlockSpec((sk, kw), lambda b, g, i: (b, g)),
        ],
        out_specs=pl.BlockSpec((tq, gw), lambda b, g, i: (b * nq + i, g)),
        out_shape=jax.ShapeDtypeStruct((batch * sq, N_Q_HEADS * HEAD_DIM), BF16),
        compiler_params=_cparams(("parallel", "parallel", "arbitrary")),
        name="gqa_attention",
    )(q, k, v)


def _oproj_kernel(x_ref, mod_ref, a_ref, w_ref, o_ref):
    y = jnp.dot(a_ref[...], w_ref[...], preferred_element_type=F32)
    o_ref[...] = x_ref[...] + mod_ref[...] * y


def _attn_out(x, gate, a, w_o):
    m, d = x.shape
    tm = ATTN_OUT_TM
    return pl.pallas_call(
        _oproj_kernel,
        grid=(m // tm,),
        in_specs=[
            pl.BlockSpec((tm, d), lambda i: (i, 0)),
            pl.BlockSpec((None, 1, d), _mod_index(tm)),
            pl.BlockSpec((tm, a.shape[1]), lambda i: (i, 0)),
            pl.BlockSpec(w_o.shape, lambda i: (0, 0)),
        ],
        out_specs=pl.BlockSpec((tm, d), lambda i: (i, 0)),
        out_shape=jax.ShapeDtypeStruct((m, d), F32),
        compiler_params=_cparams(("parallel",)),
        name="attn_out",
    )(x, gate, a, w_o)


def _permute_w_in(w):
    w = w.astype(BF16)
    pad = jnp.zeros((w.shape[0], AB_PAD_COLS - 6192), BF16)
    return jnp.concatenate([w[:, :3072], w[:, 3088:6160], w[:, 3072:3088], w[:, 6160:6192], pad], axis=1)


def _rope_tables(n_tokens):
    t = jnp.arange(n_tokens)
    row = (t // GRID_W).astype(F32)
    col = (t % GRID_W).astype(F32)
    inv = ROPE_THETA ** (-jnp.arange(0, ROPE_AXIS, 2, dtype=F32) / ROPE_AXIS)
    ar = row[:, None] * inv
    ac = col[:, None] * inv
    cos = jnp.concatenate([jnp.cos(ar), jnp.cos(ar), jnp.cos(ac), jnp.cos(ac)], axis=1)
    sin = jnp.concatenate([-jnp.sin(ar), jnp.sin(ar), -jnp.sin(ac), jnp.sin(ac)], axis=1)
    return cos, sin


def kernel(x_prompt, x_sample, c, c_ctx, state_mlstm_C, state_mlstm_n, state_mlstm_m, state_gla_S,
           cache_k, cache_v, w_mod, b_mod, norm_g, ffn_w_gate, ffn_w_up, ffn_w_down, w_in_ab,
           mlstm_conv_w, mlstm_conv_b, mlstm_b_i, mlstm_b_f, mlstm_out_g, gla_w_gk, gla_b_gk, gla_out_g,
           w_out_ab, w_qkv, q_norm_g, k_norm_g, w_o):
    bp, sp, d = x_prompt.shape
    bs, ss, _ = x_sample.shape
    depth = w_mod.shape[0]
    xp = x_prompt.reshape(bp * sp, d)
    xs = x_sample.reshape(bs * ss, d)

    c8 = jnp.concatenate([c_ctx[None, :], c, jnp.zeros((8 - 1 - bs, d), F32)], axis=0)
    mod = _modulation(c8, w_mod, b_mod).reshape(depth, 8, N_MOD, d)

    ffn_w = (ffn_w_gate.astype(BF16), ffn_w_up.astype(BF16), ffn_w_down.astype(BF16))
    new_c, new_n, new_m, new_s, new_k, new_v = [], [], [], [], [], []
    for l in range(depth):
        mod_p = mod[l, 0:1]
        mod_s = mod[l, 1:1 + bs]

        def ffn_half(x, md, j, half):
            return _ffn(x, md[:, 3 * j:3 * j + 3], norm_g[l, j][None, :], *ffn_w, l, half)

        xp = ffn_half(xp, mod_p, 0, 0)
        xs = ffn_half(xs, mod_s, 0, 0)
        g_mix = norm_g[l, 1][None, :]
        if l % 2 == 0:
            e = l // 2
            w_in = _permute_w_in(w_in_ab[e])
            w_out = w_out_ab[e].astype(BF16)
            gbias = jnp.zeros((2, 2, HEADS), F32)
            gbias = gbias.at[:, 0].set(mlstm_b_i[e]).at[:, 1].set(mlstm_b_f[e])
            gbias = jnp.pad(gbias.reshape(1, 16), ((0, 0), (0, 128 - 16)))
            wgk = jnp.zeros((2, 128, HEADS * DQK), F32)
            for j in range(2):
                lo = LR_OFF + j * GLA_RANK
                wgk = wgk.at[j, lo:lo + GLA_RANK].set(gla_w_gk[e, j])
            wgk = wgk.astype(BF16)
            bgk = gla_b_gk[e][:, None, :]
            conv_w = mlstm_conv_w[e]
            conv_b = mlstm_conv_b[e][None, :]
            a_gain = mlstm_out_g[e][None, :]
            b_gain = gla_out_g[e][None, :]

            def mixer(x, md, batch, seq, c0, n0, m0, s0, emit_state):
                proj = _proj(x, md[:, 3:6], g_mix, w_in, 1280)
                q, k = _conv_qk(proj, conv_w, conv_b, seq)
                m0r = jnp.broadcast_to(m0[..., None], m0.shape + (128,))
                r = _mixer_scan(q, k, proj, gbias, wgk, bgk, c0, n0, m0r, s0,
                                batch=batch, seq=seq, emit_state=emit_state)
                y = _mixer_out(x, md[:, 5:6], r[0], r[1], r[2], r[3], proj, a_gain, b_gain, w_out)
                return y, r[4:]

            zc = jnp.zeros((bp, 2, HEADS, DQK, DV), F32)
            zn = jnp.zeros((bp, 2, HEADS, DQK), F32)
            zm = jnp.zeros((bp, 2, HEADS), F32)
            xp, st = mixer(xp, mod_p, bp, sp, zc, zn, zm, zc, True)
            xs, _ = mixer(xs, mod_s, bs, ss, state_mlstm_C[:, e], state_mlstm_n[:, e],
                          state_mlstm_m[:, e], state_gla_S[:, e], False)
            new_c.append(st[0])
            new_n.append(st[1])
            new_m.append(st[2][..., 0])
            new_s.append(st[3])
        else:
            o = l // 2
            wq = w_qkv[o].astype(BF16)
            wo = w_o[o].astype(BF16)
            qg = q_norm_g[o][None, :]
            kg = k_norm_g[o][None, :]
            nk = HEADS * HEAD_DIM
            proj_p = _proj(xp, mod_p[:, 3:6], g_mix, wq, 1024)
            q, k, v, kf = _qk_prep(proj_p, qg, kg, None)
            a = _attention(q, k, v, batch=bp, sq=sp, sk=sp)
            xp = _attn_out(xp, mod_p[:, 5:6], a, wo)
            new_k.append(kf.reshape(bp, sp, HEADS, HEAD_DIM))
            new_v.append(proj_p[:, N_Q_HEADS * HEAD_DIM + nk:].reshape(bp, sp, HEADS, HEAD_DIM))
            proj_s = _proj(xs, mod_s[:, 3:6], g_mix, wq, 1024)
            q, k, v = _qk_prep(proj_s, qg, kg, _rope_tables(ss))
            past = cache_k.shape[2]
            ck = cache_k[:, o].reshape(bs, past, nk).astype(BF16)
            cv = cache_v[:, o].reshape(bs, past, nk).astype(BF16)
            k_all = jnp.concatenate([ck, k.reshape(bs, ss, nk)], axis=1).reshape(bs * (past + ss), nk)
            v_all = jnp.concatenate([cv, v.reshape(bs, ss, nk)], axis=1).reshape(bs * (past + ss), nk)
            a = _attention(q, k_all, v_all, batch=bs, sq=ss, sk=past + ss)
            xs = _attn_out(xs, mod_s[:, 5:6], a, wo)
        xp = ffn_half(xp, mod_p, 2, 1)
        xs = ffn_half(xs, mod_s, 2, 1)

    return (xp.reshape(bp, sp, d), xs.reshape(bs, ss, d),
            jnp.stack(new_c, axis=1), jnp.stack(new_n, axis=1), jnp.stack(new_m, axis=1),
            jnp.stack(new_s, axis=1), jnp.stack(new_k, axis=1), jnp.stack(new_v, axis=1))
```

```python
import functools

import jax
import jax.numpy as jnp
from jax import lax
from jax.experimental import pallas as pl
from jax.experimental.pallas import tpu as pltpu

F32 = jnp.float32
BF16 = jnp.bfloat16

D_MODEL = 2048
N_MOD = 9
D_FF = 5632
RMS_EPS = 1e-6
ROWS_PER_MOD = 4096
HEADS = 4
DQK = 128
DV = 256
GLA_RANK = 16
GLA_GATE_NORM = 16.0
CONV_W = 3
N_Q_HEADS = 16
Q_PER_KV = 4
HEAD_DIM = 128
GRID_W = 64
ROPE_AXIS = HEAD_DIM // 2
ROPE_THETA = 10000.0

COL_AQ, COL_AK, COL_AV, COL_AO = 0, 512, 1024, 2048
COL_BQ, COL_BK, COL_BV, COL_BG = 3072, 3584, 4096, 5120
COL_SMALL = 6144
AB_PAD_COLS = 6400
LR_OFF = 16

VMEM_LIMIT = 60 * 1024 * 1024
SCAN_L = 128
FFN_TM, FFN_TF = 1024, 512
PROJ_TM = 1024
ATTN_OUT_TM = 512
OUT_TM = 256
PREP_TR = 256
ATT_TQ = 128
ATT_KVG = 2
ATT_KC = 256
LOG2E = 1.4426950408889634


def _cparams(sem):
    return pltpu.CompilerParams(dimension_semantics=sem, vmem_limit_bytes=VMEM_LIMIT)


def _log_sigmoid(x):
    return jnp.minimum(x, 0.0) - jnp.log1p(jnp.exp(-jnp.abs(x)))


def _adaln(x, g, shift, scale):
    ms = jnp.mean(x * x, axis=-1, keepdims=True)
    return (x * lax.rsqrt(ms + RMS_EPS) * g) * (1.0 + scale) + shift


def _mod_index(tm):
    return lambda i, *_: ((i * tm) // ROWS_PER_MOD, 0, 0)


def _mod_kernel(c_ref, w_ref, b_ref, o_ref):
    c = c_ref[...]
    s = (c * jax.nn.sigmoid(c)).astype(BF16)
    o_ref[...] = jnp.dot(s, w_ref[...].astype(BF16), preferred_element_type=F32) + b_ref[...]


def _modulation(c8, w_mod, b_mod):
    depth, d, n = w_mod.shape
    tn = 1024
    return pl.pallas_call(
        _mod_kernel,
        grid=(depth, n // tn),
        in_specs=[
            pl.BlockSpec((8, d), lambda l, j: (0, 0)),
            pl.BlockSpec((None, d, tn), lambda l, j: (l, 0, j)),
            pl.BlockSpec((None, 1, tn), lambda l, j: (l, 0, j)),
        ],
        out_specs=pl.BlockSpec((None, 8, tn), lambda l, j: (l, 0, j)),
        out_shape=jax.ShapeDtypeStruct((depth, 8, n), F32),
        compiler_params=_cparams(("arbitrary", "arbitrary")),
        name="modulation",
    )(c8, w_mod, b_mod.reshape(depth, 1, n))


def _ffn_kernel(x_ref, mod_ref, g_ref, wg_ref, wu_ref, wd_ref, o_ref, h_ref, *, nf):
    f = pl.program_id(1)

    def swiglu_part(h):
        gate = jnp.dot(h, wg_ref[...], preferred_element_type=F32)
        up = jnp.dot(h, wu_ref[...], preferred_element_type=F32)
        a = (gate * jax.nn.sigmoid(gate) * up).astype(BF16)
        return jnp.dot(a, wd_ref[...], preferred_element_type=F32)

    @pl.when(f == 0)
    def _():
        h = _adaln(x_ref[...], g_ref[...], mod_ref[0:1, :], mod_ref[1:2, :]).astype(BF16)
        h_ref[...] = h
        o_ref[...] = swiglu_part(h)

    @pl.when(jnp.logical_and(f > 0, f < nf - 1))
    def _():
        o_ref[...] += swiglu_part(h_ref[...])

    @pl.when(f == nf - 1)
    def _():
        acc = o_ref[...] + swiglu_part(h_ref[...])
        o_ref[...] = x_ref[...] + (0.5 * mod_ref[2:3, :]) * acc


def _ffn(x, mod3, g, wg, wu, wd, layer, half):
    m, d = x.shape
    tm, tf = FFN_TM, FFN_TF
    nf = D_FF // tf
    return pl.pallas_call(
        functools.partial(_ffn_kernel, nf=nf),
        grid=(m // tm, nf),
        in_specs=[
            pl.BlockSpec((tm, d), lambda i, f: (i, 0)),
            pl.BlockSpec((None, 3, d), _mod_index(tm)),
            pl.BlockSpec((1, d), lambda i, f: (0, 0)),
            pl.BlockSpec((None, None, d, tf), lambda i, f: (layer, half, 0, f)),
            pl.BlockSpec((None, None, d, tf), lambda i, f: (layer, half, 0, f)),
            pl.BlockSpec((None, None, tf, d), lambda i, f: (layer, half, f, 0)),
        ],
        out_specs=pl.BlockSpec((tm, d), lambda i, f: (i, 0)),
        out_shape=jax.ShapeDtypeStruct((m, d), F32),
        scratch_shapes=[pltpu.VMEM((tm, d), BF16)],
        compiler_params=_cparams(("parallel", "arbitrary")),
        name="ffn",
    )(x, mod3, g, wg, wu, wd)


def _proj_kernel(x_ref, mod_ref, g_ref, w_ref, o_ref, h_ref):
    j = pl.program_id(1)

    @pl.when(j == 0)
    def _():
        h = _adaln(x_ref[...], g_ref[...], mod_ref[0:1, :], mod_ref[1:2, :]).astype(BF16)
        h_ref[...] = h
        o_ref[...] = jnp.dot(h, w_ref[...], preferred_element_type=F32)

    @pl.when(j > 0)
    def _():
        o_ref[...] = jnp.dot(h_ref[...], w_ref[...], preferred_element_type=F32)


def _proj(x, mod3, g, w, tn):
    m, d = x.shape
    n = w.shape[1]
    tm = PROJ_TM
    return pl.pallas_call(
        _proj_kernel,
        grid=(m // tm, n // tn),
        in_specs=[
            pl.BlockSpec((tm, d), lambda i, j: (i, 0)),
            pl.BlockSpec((None, 3, d), _mod_index(tm)),
            pl.BlockSpec((1, d), lambda i, j: (0, 0)),
            pl.BlockSpec((d, tn), lambda i, j: (0, j)),
        ],
        out_specs=pl.BlockSpec((tm, tn), lambda i, j: (i, j)),
        out_shape=jax.ShapeDtypeStruct((m, n), F32),
        scratch_shapes=[pltpu.VMEM((tm, d), BF16)],
        compiler_params=_cparams(("parallel", "arbitrary")),
        name="adaln_proj",
    )(x, mod3, g, w)


def _conv_kernel(x_ref, prev_ref, next_ref, w_ref, b_ref, q_ref, k_ref, *, tr, seq):
    i = pl.program_id(0)
    x = x_ref[...]
    first = (i * tr) % seq == 0
    last = ((i + 1) * tr) % seq == 0
    prev_row = jnp.where(first, 0.0, prev_ref[7:8, :])
    next_row = jnp.where(last, 0.0, next_ref[0:1, :])
    ridx = lax.broadcasted_iota(jnp.int32, x.shape, 0)
    xm = jnp.where(ridx == 0, prev_row, pltpu.roll(x, 1, 0))
    xp = jnp.where(ridx == tr - 1, next_row, pltpu.roll(x, tr - 1, 0))
    y = xm * w_ref[0:1, :] + x * w_ref[1:2, :] + xp * w_ref[2:3, :] + b_ref[...]
    y = y * jax.nn.sigmoid(y)
    hqk = HEADS * DQK
    q_ref[...] = y[:, :hqk].astype(BF16)
    k_ref[...] = (y[:, hqk:] * (DQK ** -0.5)).astype(BF16)


def _conv_qk(proj, conv_w, conv_b, seq):
    m = proj.shape[0]
    tr = PREP_TR
    c = 2 * HEADS * DQK
    nb8 = m // 8
    return pl.pallas_call(
        functools.partial(_conv_kernel, tr=tr, seq=seq),
        grid=(m // tr,),
        in_specs=[
            pl.BlockSpec((tr, c), lambda i: (i, 0)),
            pl.BlockSpec((8, c), lambda i: (jnp.maximum(i * (tr // 8) - 1, 0), 0)),
            pl.BlockSpec((8, c), lambda i: (jnp.minimum((i + 1) * (tr // 8), nb8 - 1), 0)),
            pl.BlockSpec((CONV_W, c), lambda i: (0, 0)),
            pl.BlockSpec((1, c), lambda i: (0, 0)),
        ],
        out_specs=[
            pl.BlockSpec((tr, c // 2), lambda i: (i, 0)),
            pl.BlockSpec((tr, c // 2), lambda i: (i, 0)),
        ],
        out_shape=[jax.ShapeDtypeStruct((m, c // 2), BF16)] * 2,
        compiler_params=_cparams(("parallel",)),
        name="mlstm_conv",
    )(proj, proj, proj, conv_w, conv_b)


def _order_mask(n, rev):
    t = lax.broadcasted_iota(jnp.int32, (n, n), 0)
    s = lax.broadcasted_iota(jnp.int32, (n, n), 1)
    return (s >= t) if rev else (s <= t)


def _masked_sum(mask, x):
    n = x.shape[1]
    hi = x.astype(BF16)
    r = x - hi.astype(F32)
    mid = r.astype(BF16)
    lo = (r - mid.astype(F32)).astype(BF16)
    y = jnp.dot(mask.astype(BF16), jnp.concatenate([hi, mid, lo], axis=1), preferred_element_type=F32)
    return y[:, :n] + y[:, n:2 * n] + y[:, 2 * n:]


def _col_rep(row):
    return jnp.broadcast_to(row, (128, 128)).T


def _dot_nt(a, b):
    return lax.dot_general(a, b, (((1,), (1,)), ((), ())), preferred_element_type=F32)


def _dot_tn(a, b):
    return lax.dot_general(a, b, (((0,), (0,)), ((), ())), preferred_element_type=F32)


def _wide(x):
    return jnp.concatenate([x, x], axis=1)


def _mlstm_chain(q_ref, k_ref, v_ref, g_ref, gb_ref, h_ref, c_ref, n_ref, m_ref, *, direction):
    L = SCAN_L
    rev = direction == 1
    mask = _order_mask(L, rev)
    gates = g_ref[...] + gb_ref[...]
    logf = _log_sigmoid(gates)
    bsum = _masked_sum(mask, logf)
    col_i = [8 * direction + h for h in range(HEADS)]
    col_f = [8 * direction + HEADS + h for h in range(HEADS)]
    gates_t = gates.T
    bsum_t = bsum.T
    end = 0 if rev else L - 1

    def head(h):
        hs = slice(h * DQK, (h + 1) * DQK)
        vs = slice(h * DV, (h + 1) * DV)
        b_r = bsum_t[col_f[h]:col_f[h] + 1, :]
        i_r = gates_t[col_i[h]:col_i[h] + 1, :]
        b_c = jnp.broadcast_to(bsum[:, col_f[h]:col_f[h] + 1], (L, 128))
        i_c = jnp.broadcast_to(gates[:, col_i[h]:col_i[h] + 1], (L, 128))
        m_prev = m_ref[direction, h:h + 1, :]
        n_prev = n_ref[direction, h:h + 1, :]
        dmat = jnp.where(mask, b_c - b_r + i_r, -jnp.inf)
        inter = b_c + m_prev
        m_t = jnp.maximum(inter, jnp.max(dmat, axis=-1, keepdims=True))
        w_intra = jnp.exp(dmat - m_t)
        w_inter = jnp.exp(inter - m_t)
        qh = q_ref[:, hs]
        kh = k_ref[:, hs]
        vh = v_ref[:, vs].astype(BF16)
        s = _dot_nt(qh, kh) * w_intra
        c_old = c_ref[direction, h]
        num = _wide(w_inter) * jnp.dot(qh, c_old.astype(BF16), preferred_element_type=F32)
        num = num + jnp.dot(s.astype(BF16), vh, preferred_element_type=F32)
        qn = jnp.sum(qh.astype(F32) * n_prev, axis=-1, keepdims=True)
        den = w_inter * qn + jnp.sum(s, axis=-1, keepdims=True)
        inv = 1.0 / jnp.maximum(jnp.abs(den), jnp.exp(-m_t))
        h_ref[:, vs] = num * _wide(inv)

        g_end = b_c[end:end + 1, :]
        dec = g_end - b_c + i_c
        m_new = jnp.maximum(g_end + m_prev, jnp.max(dec, axis=0, keepdims=True))
        ws = jnp.exp(dec - m_new)
        wc = jnp.exp(g_end + m_prev - m_new)
        kw = kh.astype(F32) * ws
        c_ref[direction, h] = _wide(wc) * c_old + _dot_tn(kw.astype(BF16), vh)
        n_ref[direction, h:h + 1, :] = wc * n_prev + jnp.sum(kw, axis=0, keepdims=True)
        m_ref[direction, h:h + 1, :] = m_new

    return head


def _gla_chain(q_ref, k_ref, v_ref, g_ref, wgk_ref, bgk_ref, o_ref, s_ref, *, direction):
    L = SCAN_L
    H = L // 2
    rev = direction == 1
    glogit = jnp.dot(g_ref[...].astype(BF16), wgk_ref[direction], preferred_element_type=F32)
    la = _log_sigmoid(glogit + bgk_ref[direction]) * (1.0 / GLA_GATE_NORM)
    bc = _masked_sum(_order_mask(L, rev), la)
    mask_h = _order_mask(H, rev)
    first = slice(H, L) if rev else slice(0, H)
    second = slice(0, H) if rev else slice(H, L)
    first_end = H if rev else H - 1
    end = 0 if rev else L - 1
    scale = DQK ** -0.5

    def scores(qx, kx, bq, bk, anchor):
        qt = (qx * jnp.exp(bq - anchor)).astype(BF16)
        kt = (kx * jnp.exp(anchor - bk)).astype(BF16)
        return _dot_nt(qt, kt)

    def head(h):
        hs = slice(h * DQK, (h + 1) * DQK)
        vs = slice(h * DV, (h + 1) * DV)
        bch = bc[:, hs]
        qh = q_ref[:, hs] * scale
        kh = k_ref[:, hs]
        vh = v_ref[:, vs].astype(BF16)
        s_old = s_ref[direction, h]
        o = jnp.dot((qh * jnp.exp(bch)).astype(BF16), s_old.astype(BF16), preferred_element_type=F32)

        def diag(rows):
            mid = rows.start + H // 2
            a = scores(qh[rows], kh[rows], bch[rows], bch[rows], bch[mid:mid + 1, :])
            return jnp.where(mask_h, a, 0.0).astype(BF16)

        a_ff = diag(first)
        a_ss = diag(second)
        a_sf = scores(qh[second], kh[first], bch[second], bch[first],
                      bch[first_end:first_end + 1, :]).astype(BF16)
        o_ref[first, vs] = o[first] + jnp.dot(a_ff, vh[first], preferred_element_type=F32)
        o_ref[second, vs] = (o[second] + jnp.dot(a_sf, vh[first], preferred_element_type=F32)
                             + jnp.dot(a_ss, vh[second], preferred_element_type=F32))

        b_end = bch[end:end + 1, :]
        kdec = (kh * jnp.exp(b_end - bch)).astype(BF16)
        carry = _col_rep(jnp.exp(b_end))
        s_ref[direction, h] = _wide(carry) * s_old + _dot_tn(kdec, vh)

    return head


def _scan_kernel(*refs, nc, emit_state):
    per_dir = [refs[0:7], refs[7:14]]
    gb_ref, wgk_ref, bgk_ref, c0_ref, n0_ref, m0_ref, s0_ref = refs[14:21]
    hm_refs = refs[21:23]
    hg_refs = refs[23:25]
    rest = refs[25:]
    if emit_state:
        cf_ref, nf_ref, mf_ref, sf_ref = rest[:4]
        rest = rest[4:]
    c_ref, n_ref, m_ref, s_ref = rest
    c = pl.program_id(1)

    @pl.when(c == 0)
    def _():
        c_ref[...] = c0_ref[...]
        n_ref[...] = n0_ref[...]
        m_ref[...] = m0_ref[...]
        s_ref[...] = s0_ref[...]

    chains = []
    for direction in range(2):
        mq, mk, mv, small, gq, gk, gv = per_dir[direction]
        chains.append(_mlstm_chain(mq, mk, mv, small, gb_ref, hm_refs[direction], c_ref, n_ref, m_ref,
                                   direction=direction))
        chains.append(_gla_chain(gq, gk, gv, small, wgk_ref, bgk_ref, hg_refs[direction], s_ref,
                                 direction=direction))
    for h in range(HEADS):
        for chain_head in chains:
            chain_head(h)

    if emit_state:
        @pl.when(c == nc - 1)
        def _():
            cf_ref[...] = c_ref[...]
            nf_ref[...] = n_ref[...]
            mf_ref[...] = m_ref[...]
            sf_ref[...] = s_ref[...]


def _mixer_scan(q, k, proj, gbias, wgk, bgk, c0, n0, m0, s0, *, batch, seq, emit_state):
    L = SCAN_L
    nc = seq // L
    m = batch * seq
    qk_w, v_w = HEADS * DQK, HEADS * DV

    def dir_specs(rev):
        def row(b, c):
            return b * nc + (nc - 1 - c if rev else c)
        spec = lambda w, col: pl.BlockSpec((L, w), lambda b, c: (row(b, c), col))
        ins = [spec(qk_w, 0), spec(qk_w, 0), spec(v_w, COL_AV // v_w), spec(128, COL_SMALL // 128),
               spec(qk_w, COL_BQ // qk_w), spec(qk_w, COL_BK // qk_w), spec(v_w, COL_BV // v_w)]
        return ins, spec(v_w, 0)

    ins_f, out_f = dir_specs(False)
    ins_r, out_r = dir_specs(True)
    state = lambda *tail: pl.BlockSpec((None, 2, HEADS) + tail, lambda b, c: (b, 0, 0) + (0,) * len(tail))
    in_specs = ins_f + ins_r + [
        pl.BlockSpec((1, 128), lambda b, c: (0, 0)),
        pl.BlockSpec((2, 128, qk_w), lambda b, c: (0, 0, 0)),
        pl.BlockSpec((2, 1, qk_w), lambda b, c: (0, 0, 0)),
        state(DQK, DV), state(DQK), state(128), state(DQK, DV),
    ]
    out_specs = [out_f, out_r, out_f, out_r]
    out_shape = [jax.ShapeDtypeStruct((m, v_w), F32)] * 4
    state_shapes = [(batch, 2, HEADS, DQK, DV), (batch, 2, HEADS, DQK), (batch, 2, HEADS, 128),
                    (batch, 2, HEADS, DQK, DV)]
    if emit_state:
        out_specs += [state(DQK, DV), state(DQK), state(128), state(DQK, DV)]
        out_shape += [jax.ShapeDtypeStruct(sh, F32) for sh in state_shapes]
    per_dir_args = [q, k, proj, proj, proj, proj, proj]
    return pl.pallas_call(
        functools.partial(_scan_kernel, nc=nc, emit_state=emit_state),
        grid=(batch, nc),
        in_specs=in_specs,
        out_specs=out_specs,
        out_shape=out_shape,
        scratch_shapes=[pltpu.VMEM(sh[1:], F32) for sh in state_shapes],
        compiler_params=_cparams(("parallel", "arbitrary")),
        name="mixer_scan",
    )(*per_dir_args, *per_dir_args, gbias, wgk, bgk, c0, n0, m0, s0)


def _head_rms(x):
    parts = []
    for h in range(HEADS):
        xh = x[:, h * DV:(h + 1) * DV]
        ms = jnp.mean(xh * xh, axis=-1, keepdims=True)
        parts.append(xh * lax.rsqrt(ms + RMS_EPS))
    return jnp.concatenate(parts, axis=1)


def _mixout_kernel(x_ref, mod_ref, hmf_ref, hmr_ref, hgf_ref, hgr_ref, ao_ref, bg_ref,
                   ag_ref, bgn_ref, w_ref, o_ref):
    ya = _head_rms(hmf_ref[...] + hmr_ref[...]) * ag_ref[...] * jax.nn.sigmoid(ao_ref[...])
    bg = bg_ref[...]
    yb = _head_rms(hgf_ref[...] + hgr_ref[...]) * bgn_ref[...] * (bg * jax.nn.sigmoid(bg))
    hw = HEADS * DV
    y = jnp.dot(ya.astype(BF16), w_ref[0:hw, :], preferred_element_type=F32)
    y = y + jnp.dot(yb.astype(BF16), w_ref[hw:2 * hw, :], preferred_element_type=F32)
    o_ref[...] = x_ref[...] + mod_ref[...] * y


def _mixer_out(x, gate, hmf, hmr, hgf, hgr, proj, a_gain, b_gain, w_out):
    m, d = x.shape
    tm = OUT_TM
    hw = HEADS * DV
    wide = lambda col: pl.BlockSpec((tm, hw), lambda i: (i, col))
    return pl.pallas_call(
        _mixout_kernel,
        grid=(m // tm,),
        in_specs=[
            pl.BlockSpec((tm, d), lambda i: (i, 0)),
            pl.BlockSpec((None, 1, d), _mod_index(tm)),
            wide(0), wide(0), wide(0), wide(0),
            wide(COL_AO // hw), wide(COL_BG // hw),
            pl.BlockSpec((1, hw), lambda i: (0, 0)),
            pl.BlockSpec((1, hw), lambda i: (0, 0)),
            pl.BlockSpec((2 * hw, d), lambda i: (0, 0)),
        ],
        out_specs=pl.BlockSpec((tm, d), lambda i: (i, 0)),
        out_shape=jax.ShapeDtypeStruct((m, d), F32),
        compiler_params=_cparams(("parallel",)),
        name="mixer_out",
    )(x, gate, hmf, hmr, hgf, hgr, proj, proj, a_gain, b_gain, w_out)


def _rope(x, cos, sin):
    n = x.shape[1]
    lane = lax.broadcasted_iota(jnp.int32, x.shape, 1)
    partner = jnp.where((lane % 64) < 32, pltpu.roll(x, n - 32, 1), pltpu.roll(x, 32, 1))
    reps = n // HEAD_DIM
    cos_t = jnp.concatenate([cos] * reps, axis=1) if reps > 1 else cos
    sin_t = jnp.concatenate([sin] * reps, axis=1) if reps > 1 else sin
    return x * cos_t + partner * sin_t


def _head_norm(x, g, n_heads):
    parts = []
    for h in range(n_heads):
        xh = x[:, h * HEAD_DIM:(h + 1) * HEAD_DIM]
        ms = jnp.mean(xh * xh, axis=-1, keepdims=True)
        parts.append(xh * lax.rsqrt(ms + RMS_EPS) * g)
    return jnp.concatenate(parts, axis=1)


def _qkv_kernel(x_ref, mod_ref, g_ref, w_ref, qg_ref, kg_ref, *rest, rope):
    if rope:
        cos_ref, sin_ref, q_ref, k_ref, v_ref, h_ref = rest
    else:
        q_ref, k_ref, v_ref, kf_ref, vf_ref, h_ref = rest
    j = pl.program_id(1)
    nk = HEADS * HEAD_DIM

    def finish_q(y):
        q = _head_norm(y, qg_ref[...], y.shape[1] // HEAD_DIM)
        if rope:
            q = _rope(q, cos_ref[...], sin_ref[...])
        q_ref[...] = (q * (HEAD_DIM ** -0.5 * LOG2E)).astype(BF16)

    def finish_kv(y):
        k = _head_norm(y[:, :nk], kg_ref[...], HEADS)
        v = y[:, nk:]
        if rope:
            k = _rope(k, cos_ref[...], sin_ref[...])
        else:
            kf_ref[...] = k
            vf_ref[...] = v
        k_ref[...] = k.astype(BF16)
        v_ref[...] = v.astype(BF16)

    @pl.when(j == 0)
    def _():
        h = _adaln(x_ref[...], g_ref[...], mod_ref[0:1, :], mod_ref[1:2, :]).astype(BF16)
        h_ref[...] = h
        finish_q(jnp.dot(h, w_ref[...], preferred_element_type=F32))

    @pl.when(j == 1)
    def _():
        finish_q(jnp.dot(h_ref[...], w_ref[...], preferred_element_type=F32))

    @pl.when(j == 2)
    def _():
        finish_kv(jnp.dot(h_ref[...], w_ref[...], preferred_element_type=F32))


def _qkv_proj(x, mod3, g, w, qg, kg, rope_tabs):
    m, d = x.shape
    tm = PROJ_TM
    nq = N_Q_HEADS * HEAD_DIM
    nk = HEADS * HEAD_DIM
    tn = 2 * nk
    rope = rope_tabs is not None
    in_specs = [
        pl.BlockSpec((tm, d), lambda i, j: (i, 0)),
        pl.BlockSpec((None, 3, d), _mod_index(tm)),
        pl.BlockSpec((1, d), lambda i, j: (0, 0)),
        pl.BlockSpec((d, tn), lambda i, j: (0, j)),
        pl.BlockSpec((1, HEAD_DIM), lambda i, j: (0, 0)),
        pl.BlockSpec((1, HEAD_DIM), lambda i, j: (0, 0)),
    ]
    args = [x, mod3, g, w, qg, kg]
    narrow = pl.BlockSpec((tm, nk), lambda i, j: (i, 0))
    out_specs = [pl.BlockSpec((tm, tn), lambda i, j: (i, jnp.minimum(j, nq // tn - 1))), narrow, narrow]
    out_shape = [
        jax.ShapeDtypeStruct((m, nq), BF16),
        jax.ShapeDtypeStruct((m, nk), BF16),
        jax.ShapeDtypeStruct((m, nk), BF16),
    ]
    if rope:
        nt = rope_tabs[0].shape[0] // tm
        in_specs += [pl.BlockSpec((tm, HEAD_DIM), lambda i, j: (i % nt, 0))] * 2
        args += list(rope_tabs)
    else:
        out_specs += [narrow, narrow]
        out_shape += [jax.ShapeDtypeStruct((m, nk), F32)] * 2
    return pl.pallas_call(
        functools.partial(_qkv_kernel, rope=rope),
        grid=(m // tm, (nq + 2 * nk) // tn),
        in_specs=in_specs,
        out_specs=out_specs,
        out_shape=out_shape,
        scratch_shapes=[pltpu.VMEM((tm, d), BF16)],
        compiler_params=_cparams(("parallel", "arbitrary")),
        name="qkv_proj",
    )(*args)


def _attn_kernel(q_ref, k_ref, v_ref, o_ref, sc_ref, *, tq, sk):
    kc = min(ATT_KC, sk)
    chunks = [slice(j * kc, (j + 1) * kc) for j in range(sk // kc)]
    n_chains = ATT_KVG * Q_PER_KV
    head = lambda j: slice(j * HEAD_DIM, (j + 1) * HEAD_DIM)
    kv = lambda j: head(j // Q_PER_KV)

    row_max = [None] * n_chains

    def scores_pass(j):
        qj = q_ref[:, head(j)]
        mrun = None
        for ch in chunks:
            s = _dot_nt(qj, k_ref[ch, kv(j)])
            sc_ref[j, :, ch] = s
            for b in range(kc // 128):
                sb = s[:, b * 128:(b + 1) * 128]
                mrun = sb if mrun is None else jnp.maximum(mrun, sb)
            yield
        row_max[j] = jnp.max(mrun, axis=-1, keepdims=True)

    def values_pass(j):
        m = row_max[j]
        lrun = jnp.zeros((tq, 128), F32)
        acc = jnp.zeros((tq, HEAD_DIM), F32)
        for ch in chunks:
            p = jnp.exp2(sc_ref[j, :, ch] - m)
            for b in range(kc // 128):
                lrun = lrun + p[:, b * 128:(b + 1) * 128]
            acc = acc + jnp.dot(p.astype(BF16), v_ref[ch, kv(j)], preferred_element_type=F32)
            yield
        o = acc * (1.0 / jnp.sum(lrun, axis=-1, keepdims=True))
        o_ref[:, head(j)] = o.astype(BF16)

    order = [scores_pass(0)]
    for j in range(1, n_chains):
        order += [scores_pass(j), values_pass(j - 1)]
    order.append(values_pass(n_chains - 1))
    for task in order:
        for _ in task:
            pass


def _attention(q, k, v, *, batch, sq, sk):
    tq = ATT_TQ
    nq = sq // tq
    gw = ATT_KVG * Q_PER_KV * HEAD_DIM
    kw = ATT_KVG * HEAD_DIM
    return pl.pallas_call(
        functools.partial(_attn_kernel, tq=tq, sk=sk),
        scratch_shapes=[pltpu.VMEM((ATT_KVG * Q_PER_KV, tq, sk), F32)],
        grid=(batch, HEADS // ATT_KVG, nq),
        in_specs=[
            pl.BlockSpec((tq, gw), lambda b, g, i: (b * nq + i, g)),
            pl.BlockSpec((sk, kw), lambda b, g, i: (b, g)),
            pl.BlockSpec((sk, kw), lambda b, g, i: (b, g)),
        ],
        out_specs=pl.BlockSpec((tq, gw), lambda b, g, i: (b * nq + i, g)),
        out_shape=jax.ShapeDtypeStruct((batch * sq, N_Q_HEADS * HEAD_DIM), BF16),
        compiler_params=_cparams(("parallel", "parallel", "arbitrary")),
        name="gqa_attention",
    )(q, k, v)


def _oproj_kernel(x_ref, mod_ref, a_ref, w_ref, o_ref):
    y = jnp.dot(a_ref[...], w_ref[...], preferred_element_type=F32)
    o_ref[...] = x_ref[...] + mod_ref[...] * y


def _attn_out(x, gate, a, w_o):
    m, d = x.shape
    tm = ATTN_OUT_TM
    return pl.pallas_call(
        _oproj_kernel,
        grid=(m // tm,),
        in_specs=[
            pl.BlockSpec((tm, d), lambda i: (i, 0)),
            pl.BlockSpec((None, 1, d), _mod_index(tm)),
            pl.BlockSpec((tm, a.shape[1]), lambda i: (i, 0)),
            pl.BlockSpec(w_o.shape, lambda i: (0, 0)),
        ],
        out_specs=pl.BlockSpec((tm, d), lambda i: (i, 0)),
        out_shape=jax.ShapeDtypeStruct((m, d), F32),
        compiler_params=_cparams(("parallel",)),
        name="attn_out",
    )(x, gate, a, w_o)


def _permute_w_in(w):
    w = w.astype(BF16)
    pad = jnp.zeros((w.shape[0], AB_PAD_COLS - 6192), BF16)
    return jnp.concatenate([w[:, :3072], w[:, 3088:6160], w[:, 3072:3088], w[:, 6160:6192], pad], axis=1)


def _rope_tables(n_tokens):
    t = jnp.arange(n_tokens)
    row = (t // GRID_W).astype(F32)
    col = (t % GRID_W).astype(F32)
    inv = ROPE_THETA ** (-jnp.arange(0, ROPE_AXIS, 2, dtype=F32) / ROPE_AXIS)
    ar = row[:, None] * inv
    ac = col[:, None] * inv
    cos = jnp.concatenate([jnp.cos(ar), jnp.cos(ar), jnp.cos(ac), jnp.cos(ac)], axis=1)
    sin = jnp.concatenate([-jnp.sin(ar), jnp.sin(ar), -jnp.sin(ac), jnp.sin(ac)], axis=1)
    return cos, sin


def kernel(x_prompt, x_sample, c, c_ctx, state_mlstm_C, state_mlstm_n, state_mlstm_m, state_gla_S,
           cache_k, cache_v, w_mod, b_mod, norm_g, ffn_w_gate, ffn_w_up, ffn_w_down, w_in_ab,
           mlstm_conv_w, mlstm_conv_b, mlstm_b_i, mlstm_b_f, mlstm_out_g, gla_w_gk, gla_b_gk, gla_out_g,
           w_out_ab, w_qkv, q_norm_g, k_norm_g, w_o):
    bp, sp, d = x_prompt.shape
    bs, ss, _ = x_sample.shape
    depth = w_mod.shape[0]
    xp = x_prompt.reshape(bp * sp, d)
    xs = x_sample.reshape(bs * ss, d)

    c8 = jnp.concatenate([c_ctx[None, :], c, jnp.zeros((8 - 1 - bs, d), F32)], axis=0)
    mod = _modulation(c8, w_mod, b_mod).reshape(depth, 8, N_MOD, d)

    ffn_w = (ffn_w_gate.astype(BF16), ffn_w_up.astype(BF16), ffn_w_down.astype(BF16))
    new_c, new_n, new_m, new_s, new_k, new_v = [], [], [], [], [], []
    for l in range(depth):
        mod_p = mod[l, 0:1]
        mod_s = mod[l, 1:1 + bs]

        def ffn_half(x, md, j, half):
            return _ffn(x, md[:, 3 * j:3 * j + 3], norm_g[l, j][None, :], *ffn_w, l, half)

        xp = ffn_half(xp, mod_p, 0, 0)
        xs = ffn_half(xs, mod_s, 0, 0)
        g_mix = norm_g[l, 1][None, :]
        if l % 2 == 0:
            e = l // 2
            w_in = _permute_w_in(w_in_ab[e])
            w_out = w_out_ab[e].astype(BF16)
            gbias = jnp.zeros((2, 2, HEADS), F32)
            gbias = gbias.at[:, 0].set(mlstm_b_i[e]).at[:, 1].set(mlstm_b_f[e])
            gbias = jnp.pad(gbias.reshape(1, 16), ((0, 0), (0, 128 - 16)))
            wgk = jnp.zeros((2, 128, HEADS * DQK), F32)
            for j in range(2):
                lo = LR_OFF + j * GLA_RANK
                wgk = wgk.at[j, lo:lo + GLA_RANK].set(gla_w_gk[e, j])
            wgk = wgk.astype(BF16)
            bgk = gla_b_gk[e][:, None, :]
            conv_w = mlstm_conv_w[e]
            conv_b = mlstm_conv_b[e][None, :]
            a_gain = mlstm_out_g[e][None, :]
            b_gain = gla_out_g[e][None, :]

            def mixer(x, md, batch, seq, c0, n0, m0, s0, emit_state):
                proj = _proj(x, md[:, 3:6], g_mix, w_in, 1280)
                q, k = _conv_qk(proj, conv_w, conv_b, seq)
                m0r = jnp.broadcast_to(m0[..., None], m0.shape + (128,))
                r = _mixer_scan(q, k, proj, gbias, wgk, bgk, c0, n0, m0r, s0,
                                batch=batch, seq=seq, emit_state=emit_state)
                y = _mixer_out(x, md[:, 5:6], r[0], r[1], r[2], r[3], proj, a_gain, b_gain, w_out)
                return y, r[4:]

            zc = jnp.zeros((bp, 2, HEADS, DQK, DV), F32)
            zn = jnp.zeros((bp, 2, HEADS, DQK), F32)
            zm = jnp.zeros((bp, 2, HEADS), F32)
            xp, st = mixer(xp, mod_p, bp, sp, zc, zn, zm, zc, True)
            xs, _ = mixer(xs, mod_s, bs, ss, state_mlstm_C[:, e], state_mlstm_n[:, e],
                          state_mlstm_m[:, e], state_gla_S[:, e], False)
            new_c.append(st[0])
            new_n.append(st[1])
            new_m.append(st[2][..., 0])
            new_s.append(st[3])
        else:
            o = l // 2
            wq = w_qkv[o].astype(BF16)
            wo = w_o[o].astype(BF16)
            qg = q_norm_g[o][None, :]
            kg = k_norm_g[o][None, :]
            nk = HEADS * HEAD_DIM
            q, k, v, kf, vf = _qkv_proj(xp, mod_p[:, 3:6], g_mix, wq, qg, kg, None)
            a = _attention(q, k, v, batch=bp, sq=sp, sk=sp)
            xp = _attn_out(xp, mod_p[:, 5:6], a, wo)
            new_k.append(kf.reshape(bp, sp, HEADS, HEAD_DIM))
            new_v.append(vf.reshape(bp, sp, HEADS, HEAD_DIM))
            q, k, v = _qkv_proj(xs, mod_s[:, 3:6], g_mix, wq, qg, kg, _rope_tables(ss))
            past = cache_k.shape[2]
            ck = cache_k[:, o].reshape(bs, past, nk).astype(BF16)
            cv = cache_v[:, o].reshape(bs, past, nk).astype(BF16)
            k_all = jnp.concatenate([ck, k.reshape(bs, ss, nk)], axis=1).reshape(bs * (past + ss), nk)
            v_all = jnp.concatenate([cv, v.reshape(bs, ss, nk)], axis=1).reshape(bs * (past + ss), nk)
            a = _attention(q, k_all, v_all, batch=bs, sq=ss, sk=past + ss)
            xs = _attn_out(xs, mod_s[:, 5:6], a, wo)
        xp = ffn_half(xp, mod_p, 2, 1)
        xs = ffn_half(xs, mod_s, 2, 1)

    return (xp.reshape(bp, sp, d), xs.reshape(bs, ss, d),
            jnp.stack(new_c, axis=1), jnp.stack(new_n, axis=1), jnp.stack(new_m, axis=1),
            jnp.stack(new_s, axis=1), jnp.stack(new_k, axis=1), jnp.stack(new_v, axis=1))
```

```python
import functools

import jax
import jax.numpy as jnp
from jax import lax
from jax.experimental import pallas as pl
from jax.experimental.pallas import tpu as pltpu

F32 = jnp.float32
BF16 = jnp.bfloat16

D_MODEL = 2048
N_MOD = 9
D_FF = 5632
RMS_EPS = 1e-6
ROWS_PER_MOD = 4096
HEADS = 4
DQK = 128
DV = 256
GLA_RANK = 16
GLA_GATE_NORM = 16.0
CONV_W = 3
N_Q_HEADS = 16
Q_PER_KV = 4
HEAD_DIM = 128
GRID_W = 64
ROPE_AXIS = HEAD_DIM // 2
ROPE_THETA = 10000.0

COL_AQ, COL_AK, COL_AV, COL_AO = 0, 512, 1024, 2048
COL_BQ, COL_BK, COL_BV, COL_BG = 3072, 3584, 4096, 5120
COL_SMALL = 6144
AB_PAD_COLS = 6400
LR_OFF = 16

VMEM_LIMIT = 60 * 1024 * 1024
SCAN_L = 128
SCAN_NB = 2
FFN_TM, FFN_TF = 1024, 512
PROJ_TM = 1024
ATTN_OUT_TM = 512
OUT_TM = 256
ATT_TQ = 128
ATT_KVG = 2
ATT_SKEW = 1
ATT_KC = 256
LOG2E = 1.4426950408889634


def _cparams(sem):
    return pltpu.CompilerParams(dimension_semantics=sem, vmem_limit_bytes=VMEM_LIMIT)


def _log_sigmoid(x):
    return jnp.minimum(x, 0.0) - jnp.log1p(jnp.exp(-jnp.abs(x)))


def _adaln(x, g, shift, scale):
    ms = jnp.mean(x * x, axis=-1, keepdims=True)
    return (x * lax.rsqrt(ms + RMS_EPS) * g) * (1.0 + scale) + shift


def _mod_index(tm):
    return lambda i, *_: ((i * tm) // ROWS_PER_MOD, 0, 0)


def _mod_kernel(c_ref, w_ref, b_ref, o_ref):
    c = c_ref[...]
    s = (c * jax.nn.sigmoid(c)).astype(BF16)
    o_ref[...] = jnp.dot(s, w_ref[...].astype(BF16), preferred_element_type=F32) + b_ref[...]


def _modulation(c8, w_mod, b_mod):
    depth, d, n = w_mod.shape
    tn = 1024
    return pl.pallas_call(
        _mod_kernel,
        grid=(depth, n // tn),
        in_specs=[
            pl.BlockSpec((8, d), lambda l, j: (0, 0)),
            pl.BlockSpec((None, d, tn), lambda l, j: (l, 0, j)),
            pl.BlockSpec((None, 1, tn), lambda l, j: (l, 0, j)),
        ],
        out_specs=pl.BlockSpec((None, 8, tn), lambda l, j: (l, 0, j)),
        out_shape=jax.ShapeDtypeStruct((depth, 8, n), F32),
        compiler_params=_cparams(("arbitrary", "arbitrary")),
        name="modulation",
    )(c8, w_mod, b_mod.reshape(depth, 1, n))


def _ffn_kernel(x_ref, mod_ref, g_ref, wg_ref, wu_ref, wd_ref, o_ref, h_ref, *, nf):
    f = pl.program_id(1)

    def swiglu_part(h):
        gate = jnp.dot(h, wg_ref[...], preferred_element_type=F32)
        up = jnp.dot(h, wu_ref[...], preferred_element_type=F32)
        a = (gate * jax.nn.sigmoid(gate) * up).astype(BF16)
        return jnp.dot(a, wd_ref[...], preferred_element_type=F32)

    @pl.when(f == 0)
    def _():
        h = _adaln(x_ref[...], g_ref[...], mod_ref[0:1, :], mod_ref[1:2, :]).astype(BF16)
        h_ref[...] = h
        o_ref[...] = swiglu_part(h)

    @pl.when(jnp.logical_and(f > 0, f < nf - 1))
    def _():
        o_ref[...] += swiglu_part(h_ref[...])

    @pl.when(f == nf - 1)
    def _():
        acc = o_ref[...] + swiglu_part(h_ref[...])
        o_ref[...] = x_ref[...] + (0.5 * mod_ref[2:3, :]) * acc


def _ffn(x, mod3, g, wg, wu, wd, layer, half):
    m, d = x.shape
    tm, tf = FFN_TM, FFN_TF
    nf = D_FF // tf
    return pl.pallas_call(
        functools.partial(_ffn_kernel, nf=nf),
        grid=(m // tm, nf),
        in_specs=[
            pl.BlockSpec((tm, d), lambda i, f: (i, 0)),
            pl.BlockSpec((None, 3, d), _mod_index(tm)),
            pl.BlockSpec((1, d), lambda i, f: (0, 0)),
            pl.BlockSpec((None, None, d, tf), lambda i, f: (layer, half, 0, f)),
            pl.BlockSpec((None, None, d, tf), lambda i, f: (layer, half, 0, f)),
            pl.BlockSpec((None, None, tf, d), lambda i, f: (layer, half, f, 0)),
        ],
        out_specs=pl.BlockSpec((tm, d), lambda i, f: (i, 0)),
        out_shape=jax.ShapeDtypeStruct((m, d), F32),
        scratch_shapes=[pltpu.VMEM((tm, d), BF16)],
        compiler_params=_cparams(("parallel", "arbitrary")),
        name="ffn",
    )(x, mod3, g, wg, wu, wd)


def _proj_kernel(x_ref, mod_ref, g_ref, w_ref, o_ref, h_ref):
    j = pl.program_id(1)

    @pl.when(j == 0)
    def _():
        h = _adaln(x_ref[...], g_ref[...], mod_ref[0:1, :], mod_ref[1:2, :]).astype(BF16)
        h_ref[...] = h
        o_ref[...] = jnp.dot(h, w_ref[...], preferred_element_type=F32)

    @pl.when(j > 0)
    def _():
        o_ref[...] = jnp.dot(h_ref[...], w_ref[...], preferred_element_type=F32)


def _proj(x, mod3, g, w, tn):
    m, d = x.shape
    n = w.shape[1]
    tm = PROJ_TM
    return pl.pallas_call(
        _proj_kernel,
        grid=(m // tm, n // tn),
        in_specs=[
            pl.BlockSpec((tm, d), lambda i, j: (i, 0)),
            pl.BlockSpec((None, 3, d), _mod_index(tm)),
            pl.BlockSpec((1, d), lambda i, j: (0, 0)),
            pl.BlockSpec((d, tn), lambda i, j: (0, j)),
        ],
        out_specs=pl.BlockSpec((tm, tn), lambda i, j: (i, j)),
        out_shape=jax.ShapeDtypeStruct((m, n), F32),
        scratch_shapes=[pltpu.VMEM((tm, d), BF16)],
        compiler_params=_cparams(("parallel", "arbitrary")),
        name="adaln_proj",
    )(x, mod3, g, w)


def _conv_qk_tile(x_ref, prev_ref, next_ref, w_ref, b_ref, first, last):
    x = x_ref[...]
    n = x.shape[0]
    prev_row = jnp.where(first, 0.0, prev_ref[7:8, :])
    next_row = jnp.where(last, 0.0, next_ref[0:1, :])
    ridx = lax.broadcasted_iota(jnp.int32, x.shape, 0)
    xm = jnp.where(ridx == 0, prev_row, pltpu.roll(x, 1, 0))
    xp = jnp.where(ridx == n - 1, next_row, pltpu.roll(x, n - 1, 0))
    y = xm * w_ref[0:1, :] + x * w_ref[1:2, :] + xp * w_ref[2:3, :] + b_ref[...]
    y = y * jax.nn.sigmoid(y)
    hqk = HEADS * DQK
    return y[:, :hqk].astype(BF16), (y[:, hqk:] * (DQK ** -0.5)).astype(BF16)


def _order_mask(n, rev):
    t = lax.broadcasted_iota(jnp.int32, (n, n), 0)
    s = lax.broadcasted_iota(jnp.int32, (n, n), 1)
    return (s >= t) if rev else (s <= t)


def _masked_sum(mask, x):
    n = x.shape[1]
    hi = x.astype(BF16)
    r = x - hi.astype(F32)
    mid = r.astype(BF16)
    lo = (r - mid.astype(F32)).astype(BF16)
    y = jnp.dot(mask.astype(BF16), jnp.concatenate([hi, mid, lo], axis=1), preferred_element_type=F32)
    return y[:, :n] + y[:, n:2 * n] + y[:, 2 * n:]


def _col_rep(row):
    return jnp.broadcast_to(row, (128, 128)).T


def _dot_nt(a, b):
    return lax.dot_general(a, b, (((1,), (1,)), ((), ())), preferred_element_type=F32)


def _dot_tn(a, b):
    return lax.dot_general(a, b, (((0,), (0,)), ((), ())), preferred_element_type=F32)


def _wide(x):
    return jnp.concatenate([x, x], axis=1)


def _mlstm_chain(q, k, v_ref, g_ref, gb_ref, h_ref, c_ref, n_ref, m_ref, *, direction):
    L = SCAN_L
    rev = direction == 1
    mask = _order_mask(L, rev)
    gates = g_ref[...] + gb_ref[...]
    logf = _log_sigmoid(gates)
    bsum = _masked_sum(mask, logf)
    col_i = [8 * direction + h for h in range(HEADS)]
    col_f = [8 * direction + HEADS + h for h in range(HEADS)]
    gates_t = gates.T
    bsum_t = bsum.T
    end = 0 if rev else L - 1

    def head(h):
        hs = slice(h * DQK, (h + 1) * DQK)
        vs = slice(h * DV, (h + 1) * DV)
        b_r = bsum_t[col_f[h]:col_f[h] + 1, :]
        i_r = gates_t[col_i[h]:col_i[h] + 1, :]
        b_c = jnp.broadcast_to(bsum[:, col_f[h]:col_f[h] + 1], (L, 128))
        i_c = jnp.broadcast_to(gates[:, col_i[h]:col_i[h] + 1], (L, 128))
        m_prev = m_ref[direction, h:h + 1, :]
        n_prev = n_ref[direction, h:h + 1, :]
        dmat = jnp.where(mask, b_c - b_r + i_r, -jnp.inf)
        inter = b_c + m_prev
        m_t = jnp.maximum(inter, jnp.max(dmat, axis=-1, keepdims=True))
        w_intra = jnp.exp(dmat - m_t)
        w_inter = jnp.exp(inter - m_t)
        qh = q[:, hs]
        kh = k[:, hs]
        vh = v_ref[:, vs].astype(BF16)
        s = _dot_nt(qh, kh) * w_intra
        c_old = c_ref[direction, h]
        num = _wide(w_inter) * jnp.dot(qh, c_old.astype(BF16), preferred_element_type=F32)
        num = num + jnp.dot(s.astype(BF16), vh, preferred_element_type=F32)
        qn = jnp.sum(qh.astype(F32) * n_prev, axis=-1, keepdims=True)
        den = w_inter * qn + jnp.sum(s, axis=-1, keepdims=True)
        inv = 1.0 / jnp.maximum(jnp.abs(den), jnp.exp(-m_t))
        h_ref[:, vs] = num * _wide(inv)

        g_end = b_c[end:end + 1, :]
        dec = g_end - b_c + i_c
        m_new = jnp.maximum(g_end + m_prev, jnp.max(dec, axis=0, keepdims=True))
        ws = jnp.exp(dec - m_new)
        wc = jnp.exp(g_end + m_prev - m_new)
        kw = kh.astype(F32) * ws
        c_ref[direction, h] = _wide(wc) * c_old + _dot_tn(kw.astype(BF16), vh)
        n_ref[direction, h:h + 1, :] = wc * n_prev + jnp.sum(kw, axis=0, keepdims=True)
        m_ref[direction, h:h + 1, :] = m_new

    return head


def _gla_chain(q_ref, k_ref, v_ref, g_ref, wgk_ref, bgk_ref, o_ref, s_ref, *, direction):
    L = SCAN_L
    H = L // 2
    rev = direction == 1
    glogit = jnp.dot(g_ref[...].astype(BF16), wgk_ref[direction], preferred_element_type=F32)
    la = _log_sigmoid(glogit + bgk_ref[direction]) * (1.0 / GLA_GATE_NORM)
    bc = _masked_sum(_order_mask(L, rev), la)
    mask_h = _order_mask(H, rev)
    first = slice(H, L) if rev else slice(0, H)
    second = slice(0, H) if rev else slice(H, L)
    first_end = H if rev else H - 1
    end = 0 if rev else L - 1
    scale = DQK ** -0.5

    def scores(qx, kx, bq, bk, anchor):
        qt = (qx * jnp.exp(bq - anchor)).astype(BF16)
        kt = (kx * jnp.exp(anchor - bk)).astype(BF16)
        return _dot_nt(qt, kt)

    def head(h):
        hs = slice(h * DQK, (h + 1) * DQK)
        vs = slice(h * DV, (h + 1) * DV)
        bch = bc[:, hs]
        qh = q_ref[:, hs] * scale
        kh = k_ref[:, hs]
        vh = v_ref[:, vs].astype(BF16)
        s_old = s_ref[direction, h]
        o = jnp.dot((qh * jnp.exp(bch)).astype(BF16), s_old.astype(BF16), preferred_element_type=F32)

        def diag(rows):
            mid = rows.start + H // 2
            a = scores(qh[rows], kh[rows], bch[rows], bch[rows], bch[mid:mid + 1, :])
            return jnp.where(mask_h, a, 0.0).astype(BF16)

        a_ff = diag(first)
        a_ss = diag(second)
        a_sf = scores(qh[second], kh[first], bch[second], bch[first],
                      bch[first_end:first_end + 1, :]).astype(BF16)
        o_ref[first, vs] = o[first] + jnp.dot(a_ff, vh[first], preferred_element_type=F32)
        o_ref[second, vs] = (o[second] + jnp.dot(a_sf, vh[first], preferred_element_type=F32)
                             + jnp.dot(a_ss, vh[second], preferred_element_type=F32))

        b_end = bch[end:end + 1, :]
        kdec = (kh * jnp.exp(b_end - bch)).astype(BF16)
        carry = _col_rep(jnp.exp(b_end))
        s_ref[direction, h] = _wide(carry) * s_old + _dot_tn(kdec, vh)

    return head


def _scan_kernel(*refs, nc, emit_state):
    per_dir = [refs[0:8], refs[8:16]]
    cw_ref, cb_ref, gb_ref, wgk_ref, bgk_ref, c0_ref, n0_ref, m0_ref, s0_ref = refs[16:25]
    hm_refs = refs[25:27]
    hg_refs = refs[27:29]
    rest = refs[29:]
    if emit_state:
        cf_ref, nf_ref, mf_ref, sf_ref = rest[:4]
        rest = rest[4:]
    c_ref, n_ref, m_ref, s_ref = rest
    c = pl.program_id(1)

    @pl.when(c == 0)
    def _():
        c_ref[...] = c0_ref[...]
        n_ref[...] = n0_ref[...]
        m_ref[...] = m0_ref[...]
        s_ref[...] = s0_ref[...]

    chains = []
    for e in range(SCAN_NB):
        for direction in range(2):
            aqk, prev, nxt, mv, small, gq, gk, gv = (r.at[e] for r in per_dir[direction])
            pos = nc - 1 - c if direction == 1 else c
            mq, mk = _conv_qk_tile(aqk, prev, nxt, cw_ref, cb_ref, pos == 0, pos == nc - 1)
            chains.append(_mlstm_chain(mq, mk, mv, small, gb_ref, hm_refs[direction].at[e],
                                       c_ref.at[e], n_ref.at[e], m_ref.at[e], direction=direction))
            chains.append(_gla_chain(gq, gk, gv, small, wgk_ref, bgk_ref, hg_refs[direction].at[e],
                                     s_ref.at[e], direction=direction))
    for h in range(HEADS):
        for chain_head in chains:
            chain_head(h)

    if emit_state:
        @pl.when(c == nc - 1)
        def _():
            cf_ref[...] = c_ref[...]
            nf_ref[...] = n_ref[...]
            mf_ref[...] = m_ref[...]
            sf_ref[...] = s_ref[...]


def _mixer_scan(proj, conv_w, conv_b, gbias, wgk, bgk, c0, n0, m0, s0, *, batch, seq, emit_state):
    L = SCAN_L
    nb = SCAN_NB
    nc = seq // L
    qk_w, v_w = HEADS * DQK, HEADS * DV
    proj = proj.reshape(batch, seq, proj.shape[1])
    l8, last8 = L // 8, seq // 8 - 1

    def dir_specs(rev):
        chunk = (lambda c: nc - 1 - c) if rev else (lambda c: c)
        spec = lambda w, col: pl.BlockSpec((nb, L, w), lambda b, c: (b, chunk(c), col))
        prev = pl.BlockSpec((nb, 8, 2 * qk_w), lambda b, c: (b, jnp.maximum(chunk(c) * l8 - 1, 0), 0))
        nxt = pl.BlockSpec((nb, 8, 2 * qk_w), lambda b, c: (b, jnp.minimum((chunk(c) + 1) * l8, last8), 0))
        ins = [spec(2 * qk_w, 0), prev, nxt, spec(v_w, COL_AV // v_w), spec(128, COL_SMALL // 128),
               spec(qk_w, COL_BQ // qk_w), spec(qk_w, COL_BK // qk_w), spec(v_w, COL_BV // v_w)]
        return ins, spec(v_w, 0)

    ins_f, out_f = dir_specs(False)
    ins_r, out_r = dir_specs(True)
    state = lambda *tail: pl.BlockSpec((nb, 2, HEADS) + tail, lambda b, c: (b, 0, 0) + (0,) * len(tail))
    in_specs = ins_f + ins_r + [
        pl.BlockSpec((CONV_W, 2 * qk_w), lambda b, c: (0, 0)),
        pl.BlockSpec((1, 2 * qk_w), lambda b, c: (0, 0)),
        pl.BlockSpec((1, 128), lambda b, c: (0, 0)),
        pl.BlockSpec((2, 128, qk_w), lambda b, c: (0, 0, 0)),
        pl.BlockSpec((2, 1, qk_w), lambda b, c: (0, 0, 0)),
        state(DQK, DV), state(DQK), state(128), state(DQK, DV),
    ]
    out_specs = [out_f, out_r, out_f, out_r]
    out_shape = [jax.ShapeDtypeStruct((batch, seq, v_w), F32)] * 4
    state_shapes = [(batch, 2, HEADS, DQK, DV), (batch, 2, HEADS, DQK), (batch, 2, HEADS, 128),
                    (batch, 2, HEADS, DQK, DV)]
    if emit_state:
        out_specs += [state(DQK, DV), state(DQK), state(128), state(DQK, DV)]
        out_shape += [jax.ShapeDtypeStruct(sh, F32) for sh in state_shapes]
    per_dir_args = [proj] * 8
    outs = pl.pallas_call(
        functools.partial(_scan_kernel, nc=nc, emit_state=emit_state),
        grid=(batch // nb, nc),
        in_specs=in_specs,
        out_specs=out_specs,
        out_shape=out_shape,
        scratch_shapes=[pltpu.VMEM((nb,) + sh[1:], F32) for sh in state_shapes],
        compiler_params=_cparams(("parallel", "arbitrary")),
        name="mixer_scan",
    )(*per_dir_args, *per_dir_args, conv_w, conv_b, gbias, wgk, bgk, c0, n0, m0, s0)
    return [o.reshape(batch * seq, v_w) for o in outs[:4]] + list(outs[4:])


def _head_rms(x):
    parts = []
    for h in range(HEADS):
        xh = x[:, h * DV:(h + 1) * DV]
        ms = jnp.mean(xh * xh, axis=-1, keepdims=True)
        parts.append(xh * lax.rsqrt(ms + RMS_EPS))
    return jnp.concatenate(parts, axis=1)


def _mixout_kernel(x_ref, mod_ref, hmf_ref, hmr_ref, hgf_ref, hgr_ref, ao_ref, bg_ref,
                   ag_ref, bgn_ref, w_ref, o_ref):
    ya = _head_rms(hmf_ref[...] + hmr_ref[...]) * ag_ref[...] * jax.nn.sigmoid(ao_ref[...])
    bg = bg_ref[...]
    yb = _head_rms(hgf_ref[...] + hgr_ref[...]) * bgn_ref[...] * (bg * jax.nn.sigmoid(bg))
    hw = HEADS * DV
    y = jnp.dot(ya.astype(BF16), w_ref[0:hw, :], preferred_element_type=F32)
    y = y + jnp.dot(yb.astype(BF16), w_ref[hw:2 * hw, :], preferred_element_type=F32)
    o_ref[...] = x_ref[...] + mod_ref[...] * y


def _mixer_out(x, gate, hmf, hmr, hgf, hgr, proj, a_gain, b_gain, w_out):
    m, d = x.shape
    tm = OUT_TM
    hw = HEADS * DV
    wide = lambda col: pl.BlockSpec((tm, hw), lambda i: (i, col))
    return pl.pallas_call(
        _mixout_kernel,
        grid=(m // tm,),
        in_specs=[
            pl.BlockSpec((tm, d), lambda i: (i, 0)),
            pl.BlockSpec((None, 1, d), _mod_index(tm)),
            wide(0), wide(0), wide(0), wide(0),
            wide(COL_AO // hw), wide(COL_BG // hw),
            pl.BlockSpec((1, hw), lambda i: (0, 0)),
            pl.BlockSpec((1, hw), lambda i: (0, 0)),
            pl.BlockSpec((2 * hw, d), lambda i: (0, 0)),
        ],
        out_specs=pl.BlockSpec((tm, d), lambda i: (i, 0)),
        out_shape=jax.ShapeDtypeStruct((m, d), F32),
        compiler_params=_cparams(("parallel",)),
        name="mixer_out",
    )(x, gate, hmf, hmr, hgf, hgr, proj, proj, a_gain, b_gain, w_out)


def _rope(x, cos, sin):
    n = x.shape[1]
    lane = lax.broadcasted_iota(jnp.int32, x.shape, 1)
    partner = jnp.where((lane % 64) < 32, pltpu.roll(x, n - 32, 1), pltpu.roll(x, 32, 1))
    reps = n // HEAD_DIM
    cos_t = jnp.concatenate([cos] * reps, axis=1) if reps > 1 else cos
    sin_t = jnp.concatenate([sin] * reps, axis=1) if reps > 1 else sin
    return x * cos_t + partner * sin_t


def _head_norm(x, g, n_heads):
    parts = []
    for h in range(n_heads):
        xh = x[:, h * HEAD_DIM:(h + 1) * HEAD_DIM]
        ms = jnp.mean(xh * xh, axis=-1, keepdims=True)
        parts.append(xh * lax.rsqrt(ms + RMS_EPS) * g)
    return jnp.concatenate(parts, axis=1)


def _qkv_kernel(x_ref, mod_ref, g_ref, w_ref, qg_ref, kg_ref, *rest, rope):
    if rope:
        cos_ref, sin_ref, q_ref, k_ref, v_ref, h_ref = rest
    else:
        q_ref, k_ref, v_ref, kf_ref, vf_ref, h_ref = rest
    j = pl.program_id(1)
    nk = HEADS * HEAD_DIM

    def finish_q(y):
        q = _head_norm(y, qg_ref[...], y.shape[1] // HEAD_DIM)
        if rope:
            q = _rope(q, cos_ref[...], sin_ref[...])
        q_ref[...] = (q * (HEAD_DIM ** -0.5 * LOG2E)).astype(BF16)

    def finish_kv(y):
        k = _head_norm(y[:, :nk], kg_ref[...], HEADS)
        v = y[:, nk:]
        if rope:
            k = _rope(k, cos_ref[...], sin_ref[...])
        else:
            kf_ref[...] = k
            vf_ref[...] = v
        k_ref[...] = k.astype(BF16)
        v_ref[...] = v.astype(BF16)

    @pl.when(j == 0)
    def _():
        h = _adaln(x_ref[...], g_ref[...], mod_ref[0:1, :], mod_ref[1:2, :]).astype(BF16)
        h_ref[...] = h
        finish_q(jnp.dot(h, w_ref[...], preferred_element_type=F32))

    @pl.when(j == 1)
    def _():
        finish_q(jnp.dot(h_ref[...], w_ref[...], preferred_element_type=F32))

    @pl.when(j == 2)
    def _():
        finish_kv(jnp.dot(h_ref[...], w_ref[...], preferred_element_type=F32))


def _qkv_proj(x, mod3, g, w, qg, kg, rope_tabs):
    m, d = x.shape
    tm = PROJ_TM
    nq = N_Q_HEADS * HEAD_DIM
    nk = HEADS * HEAD_DIM
    tn = 2 * nk
    rope = rope_tabs is not None
    in_specs = [
        pl.BlockSpec((tm, d), lambda i, j: (i, 0)),
        pl.BlockSpec((None, 3, d), _mod_index(tm)),
        pl.BlockSpec((1, d), lambda i, j: (0, 0)),
        pl.BlockSpec((d, tn), lambda i, j: (0, j)),
        pl.BlockSpec((1, HEAD_DIM), lambda i, j: (0, 0)),
        pl.BlockSpec((1, HEAD_DIM), lambda i, j: (0, 0)),
    ]
    args = [x, mod3, g, w, qg, kg]
    narrow = pl.BlockSpec((tm, nk), lambda i, j: (i, 0))
    out_specs = [pl.BlockSpec((tm, tn), lambda i, j: (i, jnp.minimum(j, nq // tn - 1))), narrow, narrow]
    out_shape = [
        jax.ShapeDtypeStruct((m, nq), BF16),
        jax.ShapeDtypeStruct((m, nk), BF16),
        jax.ShapeDtypeStruct((m, nk), BF16),
    ]
    if rope:
        nt = rope_tabs[0].shape[0] // tm
        in_specs += [pl.BlockSpec((tm, HEAD_DIM), lambda i, j: (i % nt, 0))] * 2
        args += list(rope_tabs)
    else:
        out_specs += [narrow, narrow]
        out_shape += [jax.ShapeDtypeStruct((m, nk), F32)] * 2
    return pl.pallas_call(
        functools.partial(_qkv_kernel, rope=rope),
        grid=(m // tm, (nq + 2 * nk) // tn),
        in_specs=in_specs,
        out_specs=out_specs,
        out_shape=out_shape,
        scratch_shapes=[pltpu.VMEM((tm, d), BF16)],
        compiler_params=_cparams(("parallel", "arbitrary")),
        name="qkv_proj",
    )(*args)


def _attn_kernel(q_ref, k_ref, v_ref, o_ref, sc_ref, *, tq, sk):
    chunks = [slice(lo, min(lo + ATT_KC, sk)) for lo in range(0, sk, ATT_KC)]
    n_chains = ATT_KVG * Q_PER_KV
    head = lambda j: slice(j * HEAD_DIM, (j + 1) * HEAD_DIM)
    kv = lambda j: head(j // Q_PER_KV)

    row_max = [None] * n_chains

    def scores_pass(j):
        qj = q_ref[:, head(j)]
        mrun = None
        for ch in chunks:
            s = _dot_nt(qj, k_ref[ch, kv(j)])
            sc_ref[j, :, ch] = s
            for b in range(s.shape[1] // 128):
                sb = s[:, b * 128:(b + 1) * 128]
                mrun = sb if mrun is None else jnp.maximum(mrun, sb)
            yield
        row_max[j] = jnp.max(mrun, axis=-1, keepdims=True)

    def values_pass(j):
        m = row_max[j]
        lrun = jnp.zeros((tq, 128), F32)
        acc = jnp.zeros((tq, HEAD_DIM), F32)
        for ch in chunks:
            p = jnp.exp2(sc_ref[j, :, ch] - m)
            for b in range(p.shape[1] // 128):
                lrun = lrun + p[:, b * 128:(b + 1) * 128]
            acc = acc + jnp.dot(p.astype(BF16), v_ref[ch, kv(j)], preferred_element_type=F32)
            yield
        o = acc * (1.0 / jnp.sum(lrun, axis=-1, keepdims=True))
        o_ref[:, head(j)] = o.astype(BF16)

    order = [scores_pass(j) for j in range(ATT_SKEW)]
    for j in range(ATT_SKEW, n_chains):
        order += [scores_pass(j), values_pass(j - ATT_SKEW)]
    order += [values_pass(j) for j in range(n_chains - ATT_SKEW, n_chains)]
    for task in order:
        for _ in task:
            pass


def _attention(q, k, v, *, batch, sq, sk):
    tq = ATT_TQ
    nq = sq // tq
    gw = ATT_KVG * Q_PER_KV * HEAD_DIM
    kw = ATT_KVG * HEAD_DIM
    return pl.pallas_call(
        functools.partial(_attn_kernel, tq=tq, sk=sk),
        scratch_shapes=[pltpu.VMEM((ATT_KVG * Q_PER_KV, tq, sk), F32)],
        grid=(batch, HEADS // ATT_KVG, nq),
        in_specs=[
            pl.BlockSpec((tq, gw), lambda b, g, i: (b * nq + i, g)),
            pl.BlockSpec((sk, kw), lambda b, g, i: (b, g)),
            pl.BlockSpec((sk, kw), lambda b, g, i: (b, g)),
        ],
        out_specs=pl.BlockSpec((tq, gw), lambda b, g, i: (b * nq + i, g)),
        out_shape=jax.ShapeDtypeStruct((batch * sq, N_Q_HEADS * HEAD_DIM), BF16),
        compiler_params=_cparams(("parallel", "parallel", "arbitrary")),
        name="gqa_attention",
    )(q, k, v)


def _oproj_kernel(x_ref, mod_ref, a_ref, w_ref, o_ref):
    y = jnp.dot(a_ref[...], w_ref[...], preferred_element_type=F32)
    o_ref[...] = x_ref[...] + mod_ref[...] * y


def _attn_out(x, gate, a, w_o):
    m, d = x.shape
    tm = ATTN_OUT_TM
    return pl.pallas_call(
        _oproj_kernel,
        grid=(m // tm,),
        in_specs=[
            pl.BlockSpec((tm, d), lambda i: (i, 0)),
            pl.BlockSpec((None, 1, d), _mod_index(tm)),
            pl.BlockSpec((tm, a.shape[1]), lambda i: (i, 0)),
            pl.BlockSpec(w_o.shape, lambda i: (0, 0)),
        ],
        out_specs=pl.BlockSpec((tm, d), lambda i: (i, 0)),
        out_shape=jax.ShapeDtypeStruct((m, d), F32),
        compiler_params=_cparams(("parallel",)),
        name="attn_out",
    )(x, gate, a, w_o)


def _permute_w_in(w):
    w = w.astype(BF16)
    pad = jnp.zeros((w.shape[0], AB_PAD_COLS - 6192), BF16)
    return jnp.concatenate([w[:, :3072], w[:, 3088:6160], w[:, 3072:3088], w[:, 6160:6192], pad], axis=1)


def _rope_tables(n_tokens):
    t = jnp.arange(n_tokens)
    row = (t // GRID_W).astype(F32)
    col = (t % GRID_W).astype(F32)
    inv = ROPE_THETA ** (-jnp.arange(0, ROPE_AXIS, 2, dtype=F32) / ROPE_AXIS)
    ar = row[:, None] * inv
    ac = col[:, None] * inv
    cos = jnp.concatenate([jnp.cos(ar), jnp.cos(ar), jnp.cos(ac), jnp.cos(ac)], axis=1)
    sin = jnp.concatenate([-jnp.sin(ar), jnp.sin(ar), -jnp.sin(ac), jnp.sin(ac)], axis=1)
    return cos, sin


def kernel(x_prompt, x_sample, c, c_ctx, state_mlstm_C, state_mlstm_n, state_mlstm_m, state_gla_S,
           cache_k, cache_v, w_mod, b_mod, norm_g, ffn_w_gate, ffn_w_up, ffn_w_down, w_in_ab,
           mlstm_conv_w, mlstm_conv_b, mlstm_b_i, mlstm_b_f, mlstm_out_g, gla_w_gk, gla_b_gk, gla_out_g,
           w_out_ab, w_qkv, q_norm_g, k_norm_g, w_o):
    bp, sp, d = x_prompt.shape
    bs, ss, _ = x_sample.shape
    depth = w_mod.shape[0]
    xp = x_prompt.reshape(bp * sp, d)
    xs = x_sample.reshape(bs * ss, d)

    c8 = jnp.concatenate([c_ctx[None, :], c, jnp.zeros((8 - 1 - bs, d), F32)], axis=0)
    mod = _modulation(c8, w_mod, b_mod).reshape(depth, 8, N_MOD, d)

    ffn_w = (ffn_w_gate.astype(BF16), ffn_w_up.astype(BF16), ffn_w_down.astype(BF16))
    new_c, new_n, new_m, new_s, new_k, new_v = [], [], [], [], [], []
    for l in range(depth):
        mod_p = mod[l, 0:1]
        mod_s = mod[l, 1:1 + bs]

        def ffn_half(x, md, j, half):
            return _ffn(x, md[:, 3 * j:3 * j + 3], norm_g[l, j][None, :], *ffn_w, l, half)

        xp = ffn_half(xp, mod_p, 0, 0)
        xs = ffn_half(xs, mod_s, 0, 0)
        g_mix = norm_g[l, 1][None, :]
        if l % 2 == 0:
            e = l // 2
            w_in = _permute_w_in(w_in_ab[e])
            w_out = w_out_ab[e].astype(BF16)
            gbias = jnp.zeros((2, 2, HEADS), F32)
            gbias = gbias.at[:, 0].set(mlstm_b_i[e]).at[:, 1].set(mlstm_b_f[e])
            gbias = jnp.pad(gbias.reshape(1, 16), ((0, 0), (0, 128 - 16)))
            wgk = jnp.zeros((2, 128, HEADS * DQK), F32)
            for j in range(2):
                lo = LR_OFF + j * GLA_RANK
                wgk = wgk.at[j, lo:lo + GLA_RANK].set(gla_w_gk[e, j])
            wgk = wgk.astype(BF16)
            bgk = gla_b_gk[e][:, None, :]
            conv_w = mlstm_conv_w[e]
            conv_b = mlstm_conv_b[e][None, :]
            a_gain = mlstm_out_g[e][None, :]
            b_gain = gla_out_g[e][None, :]

            def mixer(x, md, batch, seq, c0, n0, m0, s0, emit_state):
                proj = _proj(x, md[:, 3:6], g_mix, w_in, 1280)
                m0r = jnp.broadcast_to(m0[..., None], m0.shape + (128,))
                r = _mixer_scan(proj, conv_w, conv_b, gbias, wgk, bgk, c0, n0, m0r, s0,
                                batch=batch, seq=seq, emit_state=emit_state)
                y = _mixer_out(x, md[:, 5:6], r[0], r[1], r[2], r[3], proj, a_gain, b_gain, w_out)
                return y, r[4:]

            zc = jnp.zeros((bp, 2, HEADS, DQK, DV), F32)
            zn = jnp.zeros((bp, 2, HEADS, DQK), F32)
            zm = jnp.zeros((bp, 2, HEADS), F32)
            xp, st = mixer(xp, mod_p, bp, sp, zc, zn, zm, zc, True)
            xs, _ = mixer(xs, mod_s, bs, ss, state_mlstm_C[:, e], state_mlstm_n[:, e],
                          state_mlstm_m[:, e], state_gla_S[:, e], False)
            new_c.append(st[0])
            new_n.append(st[1])
            new_m.append(st[2][..., 0])
            new_s.append(st[3])
        else:
            o = l // 2
            wq = w_qkv[o].astype(BF16)
            wo = w_o[o].astype(BF16)
            qg = q_norm_g[o][None, :]
            kg = k_norm_g[o][None, :]
            nk = HEADS * HEAD_DIM
            q, k, v, kf, vf = _qkv_proj(xp, mod_p[:, 3:6], g_mix, wq, qg, kg, None)
            a = _attention(q, k, v, batch=bp, sq=sp, sk=sp)
            xp = _attn_out(xp, mod_p[:, 5:6], a, wo)
            new_k.append(kf.reshape(bp, sp, HEADS, HEAD_DIM))
            new_v.append(vf.reshape(bp, sp, HEADS, HEAD_DIM))
            q, k, v = _qkv_proj(xs, mod_s[:, 3:6], g_mix, wq, qg, kg, _rope_tables(ss))
            past = cache_k.shape[2]
            ck = cache_k[:, o].reshape(bs, past, nk).astype(BF16)
            cv = cache_v[:, o].reshape(bs, past, nk).astype(BF16)
            k_all = jnp.concatenate([ck, k.reshape(bs, ss, nk)], axis=1).reshape(bs * (past + ss), nk)
            v_all = jnp.concatenate([cv, v.reshape(bs, ss, nk)], axis=1).reshape(bs * (past + ss), nk)
            a = _attention(q, k_all, v_all, batch=bs, sq=ss, sk=past + ss)
            xs = _attn_out(xs, mod_s[:, 5:6], a, wo)
        xp = ffn_half(xp, mod_p, 2, 1)
        xs = ffn_half(xs, mod_s, 2, 1)

    return (xp.reshape(bp, sp, d), xs.reshape(bs, ss, d),
            jnp.stack(new_c, axis=1), jnp.stack(new_n, axis=1), jnp.stack(new_m, axis=1),
            jnp.stack(new_s, axis=1), jnp.stack(new_k, axis=1), jnp.stack(new_v, axis=1))
```

```python
import functools

import jax
import jax.numpy as jnp
from jax import lax
from jax.experimental import pallas as pl
from jax.experimental.pallas import tpu as pltpu

F32 = jnp.float32
BF16 = jnp.bfloat16

D_MODEL = 2048
N_MOD = 9
D_FF = 5632
RMS_EPS = 1e-6
ROWS_PER_MOD = 4096
HEADS = 4
DQK = 128
DV = 256
GLA_RANK = 16
GLA_GATE_NORM = 16.0
CONV_W = 3
N_Q_HEADS = 16
Q_PER_KV = 4
HEAD_DIM = 128
GRID_W = 64
ROPE_AXIS = HEAD_DIM // 2
ROPE_THETA = 10000.0

COL_AQ, COL_AK, COL_AV, COL_AO = 0, 512, 1024, 2048
COL_BQ, COL_BK, COL_BV, COL_BG = 3072, 3584, 4096, 5120
COL_SMALL = 6144
AB_PAD_COLS = 6400
LR_OFF = 16

VMEM_LIMIT = 60 * 1024 * 1024
SCAN_L = 128
SCAN_NB = 2
FFN_TM, FFN_TF = 1024, 512
PROJ_TM = 1024
ATTN_OUT_TM = 512
OUT_TM = 256
ATT_TQ = 128
ATT_KVG = 2
ATT_SKEW = 1
ATT_KC = 256
LOG2E = 1.4426950408889634


def _cparams(sem):
    return pltpu.CompilerParams(dimension_semantics=sem, vmem_limit_bytes=VMEM_LIMIT)


def _log_sigmoid(x):
    return jnp.minimum(x, 0.0) - jnp.log1p(jnp.exp(-jnp.abs(x)))


def _adaln(x, g, shift, scale):
    ms = jnp.mean(x * x, axis=-1, keepdims=True)
    return (x * lax.rsqrt(ms + RMS_EPS) * g) * (1.0 + scale) + shift


def _mod_index(tm):
    return lambda i, *_: ((i * tm) // ROWS_PER_MOD, 0, 0)


def _mod_kernel(c_ref, w_ref, b_ref, o_ref):
    c = c_ref[...]
    s = (c * jax.nn.sigmoid(c)).astype(BF16)
    o_ref[...] = jnp.dot(s, w_ref[...].astype(BF16), preferred_element_type=F32) + b_ref[...]


def _modulation(c8, w_mod, b_mod):
    depth, d, n = w_mod.shape
    tn = 1024
    return pl.pallas_call(
        _mod_kernel,
        grid=(depth, n // tn),
        in_specs=[
            pl.BlockSpec((8, d), lambda l, j: (0, 0)),
            pl.BlockSpec((None, d, tn), lambda l, j: (l, 0, j)),
            pl.BlockSpec((None, 1, tn), lambda l, j: (l, 0, j)),
        ],
        out_specs=pl.BlockSpec((None, 8, tn), lambda l, j: (l, 0, j)),
        out_shape=jax.ShapeDtypeStruct((depth, 8, n), F32),
        compiler_params=_cparams(("arbitrary", "arbitrary")),
        name="modulation",
    )(c8, w_mod, b_mod.reshape(depth, 1, n))


def _ffn_kernel(x_ref, mod_ref, g_ref, wg_ref, wu_ref, wd_ref, o_ref, h_ref, *, nf):
    f = pl.program_id(1)

    def swiglu_part(h):
        gate = jnp.dot(h, wg_ref[...], preferred_element_type=F32)
        up = jnp.dot(h, wu_ref[...], preferred_element_type=F32)
        a = (gate * jax.nn.sigmoid(gate) * up).astype(BF16)
        return jnp.dot(a, wd_ref[...], preferred_element_type=F32)

    @pl.when(f == 0)
    def _():
        h = _adaln(x_ref[...], g_ref[...], mod_ref[0:1, :], mod_ref[1:2, :]).astype(BF16)
        h_ref[...] = h
        o_ref[...] = swiglu_part(h)

    @pl.when(jnp.logical_and(f > 0, f < nf - 1))
    def _():
        o_ref[...] += swiglu_part(h_ref[...])

    @pl.when(f == nf - 1)
    def _():
        acc = o_ref[...] + swiglu_part(h_ref[...])
        o_ref[...] = x_ref[...] + (0.5 * mod_ref[2:3, :]) * acc


def _ffn(x, mod3, g, wg, wu, wd, layer, half):
    m, d = x.shape
    tm, tf = FFN_TM, FFN_TF
    nf = D_FF // tf
    return pl.pallas_call(
        functools.partial(_ffn_kernel, nf=nf),
        grid=(m // tm, nf),
        in_specs=[
            pl.BlockSpec((tm, d), lambda i, f: (i, 0)),
            pl.BlockSpec((None, 3, d), _mod_index(tm)),
            pl.BlockSpec((1, d), lambda i, f: (0, 0)),
            pl.BlockSpec((None, None, d, tf), lambda i, f: (layer, half, 0, f)),
            pl.BlockSpec((None, None, d, tf), lambda i, f: (layer, half, 0, f)),
            pl.BlockSpec((None, None, tf, d), lambda i, f: (layer, half, f, 0)),
        ],
        out_specs=pl.BlockSpec((tm, d), lambda i, f: (i, 0)),
        out_shape=jax.ShapeDtypeStruct((m, d), F32),
        scratch_shapes=[pltpu.VMEM((tm, d), BF16)],
        compiler_params=_cparams(("parallel", "arbitrary")),
        name="ffn",
    )(x, mod3, g, wg, wu, wd)


def _proj_kernel(x_ref, mod_ref, g_ref, w_ref, o_ref, h_ref):
    j = pl.program_id(1)

    @pl.when(j == 0)
    def _():
        h = _adaln(x_ref[...], g_ref[...], mod_ref[0:1, :], mod_ref[1:2, :]).astype(BF16)
        h_ref[...] = h
        o_ref[...] = jnp.dot(h, w_ref[...], preferred_element_type=F32)

    @pl.when(j > 0)
    def _():
        o_ref[...] = jnp.dot(h_ref[...], w_ref[...], preferred_element_type=F32)


def _proj(x, mod3, g, w, tn):
    m, d = x.shape
    n = w.shape[1]
    tm = PROJ_TM
    return pl.pallas_call(
        _proj_kernel,
        grid=(m // tm, n // tn),
        in_specs=[
            pl.BlockSpec((tm, d), lambda i, j: (i, 0)),
            pl.BlockSpec((None, 3, d), _mod_index(tm)),
            pl.BlockSpec((1, d), lambda i, j: (0, 0)),
            pl.BlockSpec((d, tn), lambda i, j: (0, j)),
        ],
        out_specs=pl.BlockSpec((tm, tn), lambda i, j: (i, j)),
        out_shape=jax.ShapeDtypeStruct((m, n), F32),
        scratch_shapes=[pltpu.VMEM((tm, d), BF16)],
        compiler_params=_cparams(("parallel", "arbitrary")),
        name="adaln_proj",
    )(x, mod3, g, w)


def _conv_qk_tile(x_ref, prev_ref, next_ref, w_ref, b_ref, first, last):
    x = x_ref[...]
    n = x.shape[0]
    prev_row = jnp.where(first, 0.0, prev_ref[7:8, :])
    next_row = jnp.where(last, 0.0, next_ref[0:1, :])
    ridx = lax.broadcasted_iota(jnp.int32, x.shape, 0)
    xm = jnp.where(ridx == 0, prev_row, pltpu.roll(x, 1, 0))
    xp = jnp.where(ridx == n - 1, next_row, pltpu.roll(x, n - 1, 0))
    y = xm * w_ref[0:1, :] + x * w_ref[1:2, :] + xp * w_ref[2:3, :] + b_ref[...]
    y = y * jax.nn.sigmoid(y)
    hqk = HEADS * DQK
    return y[:, :hqk].astype(BF16), (y[:, hqk:] * (DQK ** -0.5)).astype(BF16)


def _order_mask(n, rev):
    t = lax.broadcasted_iota(jnp.int32, (n, n), 0)
    s = lax.broadcasted_iota(jnp.int32, (n, n), 1)
    return (s >= t) if rev else (s <= t)


def _masked_sum(mask, x):
    n = x.shape[1]
    hi = x.astype(BF16)
    r = x - hi.astype(F32)
    mid = r.astype(BF16)
    lo = (r - mid.astype(F32)).astype(BF16)
    y = jnp.dot(mask.astype(BF16), jnp.concatenate([hi, mid, lo], axis=1), preferred_element_type=F32)
    return y[:, :n] + y[:, n:2 * n] + y[:, 2 * n:]


def _col_rep(row):
    return jnp.broadcast_to(row, (128, 128)).T


def _dot_nt(a, b):
    return lax.dot_general(a, b, (((1,), (1,)), ((), ())), preferred_element_type=F32)


def _dot_tn(a, b):
    return lax.dot_general(a, b, (((0,), (0,)), ((), ())), preferred_element_type=F32)


def _wide(x):
    return jnp.concatenate([x, x], axis=1)


def _mlstm_chain(q, k, v_ref, g_ref, gb_ref, h_ref, c_ref, n_ref, m_ref, *, direction):
    L = SCAN_L
    rev = direction == 1
    mask = _order_mask(L, rev)
    gates = g_ref[...] + gb_ref[...]
    logf = _log_sigmoid(gates)
    bsum = _masked_sum(mask, logf)
    col_i = [8 * direction + h for h in range(HEADS)]
    col_f = [8 * direction + HEADS + h for h in range(HEADS)]
    gates_t = gates.T
    bsum_t = bsum.T
    end = 0 if rev else L - 1

    def head(h):
        hs = slice(h * DQK, (h + 1) * DQK)
        vs = slice(h * DV, (h + 1) * DV)
        b_r = bsum_t[col_f[h]:col_f[h] + 1, :]
        i_r = gates_t[col_i[h]:col_i[h] + 1, :]
        b_c = jnp.broadcast_to(bsum[:, col_f[h]:col_f[h] + 1], (L, 128))
        i_c = jnp.broadcast_to(gates[:, col_i[h]:col_i[h] + 1], (L, 128))
        m_prev = m_ref[direction, h:h + 1, :]
        n_prev = n_ref[direction, h:h + 1, :]
        dmat = jnp.where(mask, b_c - b_r + i_r, -jnp.inf)
        inter = b_c + m_prev
        m_t = jnp.maximum(inter, jnp.max(dmat, axis=-1, keepdims=True))
        w_intra = jnp.exp(dmat - m_t)
        w_inter = jnp.exp(inter - m_t)
        qh = q[:, hs]
        kh = k[:, hs]
        vh = v_ref[:, vs].astype(BF16)
        s = _dot_nt(qh, kh) * w_intra
        c_old = c_ref[direction, h]
        num = _wide(w_inter) * jnp.dot(qh, c_old.astype(BF16), preferred_element_type=F32)
        num = num + jnp.dot(s.astype(BF16), vh, preferred_element_type=F32)
        qn = jnp.sum(qh.astype(F32) * n_prev, axis=-1, keepdims=True)
        den = w_inter * qn + jnp.sum(s, axis=-1, keepdims=True)
        inv = 1.0 / jnp.maximum(jnp.abs(den), jnp.exp(-m_t))
        h_ref[:, vs] = num * _wide(inv)

        g_end = b_c[end:end + 1, :]
        dec = g_end - b_c + i_c
        m_new = jnp.maximum(g_end + m_prev, jnp.max(dec, axis=0, keepdims=True))
        ws = jnp.exp(dec - m_new)
        wc = jnp.exp(g_end + m_prev - m_new)
        kw = kh.astype(F32) * ws
        c_ref[direction, h] = _wide(wc) * c_old + _dot_tn(kw.astype(BF16), vh)
        n_ref[direction, h:h + 1, :] = wc * n_prev + jnp.sum(kw, axis=0, keepdims=True)
        m_ref[direction, h:h + 1, :] = m_new

    return head


def _gla_chain(q_ref, k_ref, v_ref, g_ref, wgk_ref, bgk_ref, o_ref, s_ref, *, direction):
    L = SCAN_L
    H = L // 2
    rev = direction == 1
    glogit = jnp.dot(g_ref[...].astype(BF16), wgk_ref[direction], preferred_element_type=F32)
    la = _log_sigmoid(glogit + bgk_ref[direction]) * (1.0 / GLA_GATE_NORM)
    bc = _masked_sum(_order_mask(L, rev), la)
    mask_h = _order_mask(H, rev)
    first = slice(H, L) if rev else slice(0, H)
    second = slice(0, H) if rev else slice(H, L)
    first_end = H if rev else H - 1
    end = 0 if rev else L - 1
    scale = DQK ** -0.5

    def scores(qx, kx, bq, bk, anchor):
        qt = (qx * jnp.exp(bq - anchor)).astype(BF16)
        kt = (kx * jnp.exp(anchor - bk)).astype(BF16)
        return _dot_nt(qt, kt)

    def head(h):
        hs = slice(h * DQK, (h + 1) * DQK)
        vs = slice(h * DV, (h + 1) * DV)
        bch = bc[:, hs]
        qh = q_ref[:, hs] * scale
        kh = k_ref[:, hs]
        vh = v_ref[:, vs].astype(BF16)
        s_old = s_ref[direction, h]
        o = jnp.dot((qh * jnp.exp(bch)).astype(BF16), s_old.astype(BF16), preferred_element_type=F32)

        def diag(rows):
            mid = rows.start + H // 2
            a = scores(qh[rows], kh[rows], bch[rows], bch[rows], bch[mid:mid + 1, :])
            return jnp.where(mask_h, a, 0.0).astype(BF16)

        a_ff = diag(first)
        a_ss = diag(second)
        a_sf = scores(qh[second], kh[first], bch[second], bch[first],
                      bch[first_end:first_end + 1, :]).astype(BF16)
        o_ref[first, vs] = o[first] + jnp.dot(a_ff, vh[first], preferred_element_type=F32)
        o_ref[second, vs] = (o[second] + jnp.dot(a_sf, vh[first], preferred_element_type=F32)
                             + jnp.dot(a_ss, vh[second], preferred_element_type=F32))

        b_end = bch[end:end + 1, :]
        kdec = (kh * jnp.exp(b_end - bch)).astype(BF16)
        carry = _col_rep(jnp.exp(b_end))
        s_ref[direction, h] = _wide(carry) * s_old + _dot_tn(kdec, vh)

    return head


def _scan_kernel(*refs, nc, emit_state):
    per_dir = [refs[0:8], refs[8:16]]
    cw_ref, cb_ref, gb_ref, wgk_ref, bgk_ref, c0_ref, n0_ref, m0_ref, s0_ref = refs[16:25]
    hm_refs = refs[25:27]
    hg_refs = refs[27:29]
    rest = refs[29:]
    if emit_state:
        cf_ref, nf_ref, mf_ref, sf_ref = rest[:4]
        rest = rest[4:]
    c_ref, n_ref, m_ref, s_ref = rest
    c = pl.program_id(1)

    @pl.when(c == 0)
    def _():
        c_ref[...] = c0_ref[...]
        n_ref[...] = n0_ref[...]
        m_ref[...] = m0_ref[...]
        s_ref[...] = s0_ref[...]

    chains = []
    for e in range(SCAN_NB):
        for direction in range(2):
            aqk, prev, nxt, mv, small, gq, gk, gv = (r.at[e] for r in per_dir[direction])
            pos = nc - 1 - c if direction == 1 else c
            mq, mk = _conv_qk_tile(aqk, prev, nxt, cw_ref, cb_ref, pos == 0, pos == nc - 1)
            chains.append(_mlstm_chain(mq, mk, mv, small, gb_ref, hm_refs[direction].at[e],
                                       c_ref.at[e], n_ref.at[e], m_ref.at[e], direction=direction))
            chains.append(_gla_chain(gq, gk, gv, small, wgk_ref, bgk_ref, hg_refs[direction].at[e],
                                     s_ref.at[e], direction=direction))
    for h in range(HEADS):
        for chain_head in chains:
            chain_head(h)

    if emit_state:
        @pl.when(c == nc - 1)
        def _():
            cf_ref[...] = c_ref[...]
            nf_ref[...] = n_ref[...]
            mf_ref[...] = m_ref[...]
            sf_ref[...] = s_ref[...]


def _mixer_scan(proj, conv_w, conv_b, gbias, wgk, bgk, c0, n0, m0, s0, *, batch, seq, emit_state):
    L = SCAN_L
    nb = SCAN_NB
    nc = seq // L
    qk_w, v_w = HEADS * DQK, HEADS * DV
    proj = proj.reshape(batch, seq, proj.shape[1])
    l8, last8 = L // 8, seq // 8 - 1

    def dir_specs(rev):
        chunk = (lambda c: nc - 1 - c) if rev else (lambda c: c)
        spec = lambda w, col: pl.BlockSpec((nb, L, w), lambda b, c: (b, chunk(c), col))
        prev = pl.BlockSpec((nb, 8, 2 * qk_w), lambda b, c: (b, jnp.maximum(chunk(c) * l8 - 1, 0), 0))
        nxt = pl.BlockSpec((nb, 8, 2 * qk_w), lambda b, c: (b, jnp.minimum((chunk(c) + 1) * l8, last8), 0))
        ins = [spec(2 * qk_w, 0), prev, nxt, spec(v_w, COL_AV // v_w), spec(128, COL_SMALL // 128),
               spec(qk_w, COL_BQ // qk_w), spec(qk_w, COL_BK // qk_w), spec(v_w, COL_BV // v_w)]
        return ins, spec(v_w, 0)

    ins_f, out_f = dir_specs(False)
    ins_r, out_r = dir_specs(True)
    state = lambda *tail: pl.BlockSpec((nb, 2, HEADS) + tail, lambda b, c: (b, 0, 0) + (0,) * len(tail))
    in_specs = ins_f + ins_r + [
        pl.BlockSpec((CONV_W, 2 * qk_w), lambda b, c: (0, 0)),
        pl.BlockSpec((1, 2 * qk_w), lambda b, c: (0, 0)),
        pl.BlockSpec((1, 128), lambda b, c: (0, 0)),
        pl.BlockSpec((2, 128, qk_w), lambda b, c: (0, 0, 0)),
        pl.BlockSpec((2, 1, qk_w), lambda b, c: (0, 0, 0)),
        state(DQK, DV), state(DQK), state(128), state(DQK, DV),
    ]
    out_specs = [out_f, out_r, out_f, out_r]
    out_shape = [jax.ShapeDtypeStruct((batch, seq, v_w), F32)] * 4
    state_shapes = [(batch, 2, HEADS, DQK, DV), (batch, 2, HEADS, DQK), (batch, 2, HEADS, 128),
                    (batch, 2, HEADS, DQK, DV)]
    if emit_state:
        out_specs += [state(DQK, DV), state(DQK), state(128), state(DQK, DV)]
        out_shape += [jax.ShapeDtypeStruct(sh, F32) for sh in state_shapes]
    per_dir_args = [proj] * 8
    outs = pl.pallas_call(
        functools.partial(_scan_kernel, nc=nc, emit_state=emit_state),
        grid=(batch // nb, nc),
        in_specs=in_specs,
        out_specs=out_specs,
        out_shape=out_shape,
        scratch_shapes=[pltpu.VMEM((nb,) + sh[1:], F32) for sh in state_shapes],
        compiler_params=_cparams(("parallel", "arbitrary")),
        name="mixer_scan",
    )(*per_dir_args, *per_dir_args, conv_w, conv_b, gbias, wgk, bgk, c0, n0, m0, s0)
    return [o.reshape(batch * seq, v_w) for o in outs[:4]] + list(outs[4:])


def _head_rms(x):
    parts = []
    for h in range(HEADS):
        xh = x[:, h * DV:(h + 1) * DV]
        ms = jnp.mean(xh * xh, axis=-1, keepdims=True)
        parts.append(xh * lax.rsqrt(ms + RMS_EPS))
    return jnp.concatenate(parts, axis=1)


def _mixout_kernel(x_ref, mod_ref, hmf_ref, hmr_ref, hgf_ref, hgr_ref, ao_ref, bg_ref,
                   ag_ref, bgn_ref, w_ref, o_ref):
    ya = _head_rms(hmf_ref[...] + hmr_ref[...]) * ag_ref[...] * jax.nn.sigmoid(ao_ref[...])
    bg = bg_ref[...]
    yb = _head_rms(hgf_ref[...] + hgr_ref[...]) * bgn_ref[...] * (bg * jax.nn.sigmoid(bg))
    hw = HEADS * DV
    y = jnp.dot(ya.astype(BF16), w_ref[0:hw, :], preferred_element_type=F32)
    y = y + jnp.dot(yb.astype(BF16), w_ref[hw:2 * hw, :], preferred_element_type=F32)
    o_ref[...] = x_ref[...] + mod_ref[...] * y


def _mixer_out(x, gate, hmf, hmr, hgf, hgr, proj, a_gain, b_gain, w_out):
    m, d = x.shape
    tm = OUT_TM
    hw = HEADS * DV
    wide = lambda col: pl.BlockSpec((tm, hw), lambda i: (i, col))
    return pl.pallas_call(
        _mixout_kernel,
        grid=(m // tm,),
        in_specs=[
            pl.BlockSpec((tm, d), lambda i: (i, 0)),
            pl.BlockSpec((None, 1, d), _mod_index(tm)),
            wide(0), wide(0), wide(0), wide(0),
            wide(COL_AO // hw), wide(COL_BG // hw),
            pl.BlockSpec((1, hw), lambda i: (0, 0)),
            pl.BlockSpec((1, hw), lambda i: (0, 0)),
            pl.BlockSpec((2 * hw, d), lambda i: (0, 0)),
        ],
        out_specs=pl.BlockSpec((tm, d), lambda i: (i, 0)),
        out_shape=jax.ShapeDtypeStruct((m, d), F32),
        compiler_params=_cparams(("parallel",)),
        name="mixer_out",
    )(x, gate, hmf, hmr, hgf, hgr, proj, proj, a_gain, b_gain, w_out)


def _rope(x, cos, sin):
    n = x.shape[1]
    lane = lax.broadcasted_iota(jnp.int32, x.shape, 1)
    partner = jnp.where((lane % 64) < 32, pltpu.roll(x, n - 32, 1), pltpu.roll(x, 32, 1))
    reps = n // HEAD_DIM
    cos_t = jnp.concatenate([cos] * reps, axis=1) if reps > 1 else cos
    sin_t = jnp.concatenate([sin] * reps, axis=1) if reps > 1 else sin
    return x * cos_t + partner * sin_t


def _head_norm(x, g, n_heads):
    parts = []
    for h in range(n_heads):
        xh = x[:, h * HEAD_DIM:(h + 1) * HEAD_DIM]
        ms = jnp.mean(xh * xh, axis=-1, keepdims=True)
        parts.append(xh * lax.rsqrt(ms + RMS_EPS) * g)
    return jnp.concatenate(parts, axis=1)


def _qkv_kernel(x_ref, mod_ref, g_ref, w_ref, qg_ref, kg_ref, *rest, rope):
    if rope:
        cos_ref, sin_ref, q_ref, k_ref, v_ref, h_ref = rest
    else:
        q_ref, k_ref, v_ref, kf_ref, vf_ref, h_ref = rest
    j = pl.program_id(1)
    nk = HEADS * HEAD_DIM

    def finish_q(y):
        q = _head_norm(y, qg_ref[...], y.shape[1] // HEAD_DIM)
        if rope:
            q = _rope(q, cos_ref[...], sin_ref[...])
        q_ref[...] = (q * (HEAD_DIM ** -0.5 * LOG2E)).astype(BF16)

    def finish_kv(y):
        k = _head_norm(y[:, :nk], kg_ref[...], HEADS)
        v = y[:, nk:]
        if rope:
            k = _rope(k, cos_ref[...], sin_ref[...])
        else:
            kf_ref[...] = k
            vf_ref[...] = v
        k_ref[...] = k.astype(BF16)
        v_ref[...] = v.astype(BF16)

    @pl.when(j == 0)
    def _():
        h = _adaln(x_ref[...], g_ref[...], mod_ref[0:1, :], mod_ref[1:2, :]).astype(BF16)
        h_ref[...] = h
        finish_q(jnp.dot(h, w_ref[...], preferred_element_type=F32))

    @pl.when(j == 1)
    def _():
        finish_q(jnp.dot(h_ref[...], w_ref[...], preferred_element_type=F32))

    @pl.when(j == 2)
    def _():
        finish_kv(jnp.dot(h_ref[...], w_ref[...], preferred_element_type=F32))


def _qkv_proj(x, mod3, g, w, qg, kg, rope_tabs):
    m, d = x.shape
    tm = PROJ_TM
    nq = N_Q_HEADS * HEAD_DIM
    nk = HEADS * HEAD_DIM
    tn = 2 * nk
    rope = rope_tabs is not None
    in_specs = [
        pl.BlockSpec((tm, d), lambda i, j: (i, 0)),
        pl.BlockSpec((None, 3, d), _mod_index(tm)),
        pl.BlockSpec((1, d), lambda i, j: (0, 0)),
        pl.BlockSpec((d, tn), lambda i, j: (0, j)),
        pl.BlockSpec((1, HEAD_DIM), lambda i, j: (0, 0)),
        pl.BlockSpec((1, HEAD_DIM), lambda i, j: (0, 0)),
    ]
    args = [x, mod3, g, w, qg, kg]
    narrow = pl.BlockSpec((tm, nk), lambda i, j: (i, 0))
    out_specs = [pl.BlockSpec((tm, tn), lambda i, j: (i, jnp.minimum(j, nq // tn - 1))), narrow, narrow]
    out_shape = [
        jax.ShapeDtypeStruct((m, nq), BF16),
        jax.ShapeDtypeStruct((m, nk), BF16),
        jax.ShapeDtypeStruct((m, nk), BF16),
    ]
    if rope:
        nt = rope_tabs[0].shape[0] // tm
        in_specs += [pl.BlockSpec((tm, HEAD_DIM), lambda i, j: (i % nt, 0))] * 2
        args += list(rope_tabs)
    else:
        out_specs += [narrow, narrow]
        out_shape += [jax.ShapeDtypeStruct((m, nk), F32)] * 2
    return pl.pallas_call(
        functools.partial(_qkv_kernel, rope=rope),
        grid=(m // tm, (nq + 2 * nk) // tn),
        in_specs=in_specs,
        out_specs=out_specs,
        out_shape=out_shape,
        scratch_shapes=[pltpu.VMEM((tm, d), BF16)],
        compiler_params=_cparams(("parallel", "arbitrary")),
        name="qkv_proj",
    )(*args)


def _attn_kernel(q_ref, k_ref, v_ref, o_ref, sc_ref, *, tq, sk):
    chunks = [slice(lo, min(lo + ATT_KC, sk)) for lo in range(0, sk, ATT_KC)]
    n_chains = ATT_KVG * Q_PER_KV
    head = lambda j: slice(j * HEAD_DIM, (j + 1) * HEAD_DIM)
    kv = lambda j: head(j // Q_PER_KV)

    row_max = [None] * n_chains

    def scores_pass(j):
        qj = q_ref[:, head(j)]
        mrun = None
        for ch in chunks:
            s = _dot_nt(qj, k_ref[ch, kv(j)])
            sc_ref[j, :, ch] = s
            for b in range(s.shape[1] // 128):
                sb = s[:, b * 128:(b + 1) * 128]
                mrun = sb if mrun is None else jnp.maximum(mrun, sb)
            yield
        row_max[j] = jnp.max(mrun, axis=-1, keepdims=True)

    def values_pass(j):
        m = row_max[j]
        lrun = jnp.zeros((tq, 128), F32)
        acc = jnp.zeros((tq, HEAD_DIM), F32)
        for ch in chunks:
            p = jnp.exp2(sc_ref[j, :, ch] - m)
            for b in range(p.shape[1] // 128):
                lrun = lrun + p[:, b * 128:(b + 1) * 128]
            acc = acc + jnp.dot(p.astype(BF16), v_ref[ch, kv(j)], preferred_element_type=F32)
            yield
        o = acc * (1.0 / jnp.sum(lrun, axis=-1, keepdims=True))
        o_ref[:, head(j)] = o.astype(BF16)

    order = [scores_pass(j) for j in range(ATT_SKEW)]
    for j in range(ATT_SKEW, n_chains):
        order += [scores_pass(j), values_pass(j - ATT_SKEW)]
    order += [values_pass(j) for j in range(n_chains - ATT_SKEW, n_chains)]
    for task in order:
        for _ in task:
            pass


def _attention(q, k, v, *, batch, sq, sk):
    tq = ATT_TQ
    nq = sq // tq
    gw = ATT_KVG * Q_PER_KV * HEAD_DIM
    kw = ATT_KVG * HEAD_DIM
    return pl.pallas_call(
        functools.partial(_attn_kernel, tq=tq, sk=sk),
        scratch_shapes=[pltpu.VMEM((ATT_KVG * Q_PER_KV, tq, sk), F32)],
        grid=(batch, HEADS // ATT_KVG, nq),
        in_specs=[
            pl.BlockSpec((tq, gw), lambda b, g, i: (b * nq + i, g)),
            pl.BlockSpec((sk, kw), lambda b, g, i: (b, g)),
            pl.BlockSpec((sk, kw), lambda b, g, i: (b, g)),
        ],
        out_specs=pl.BlockSpec((tq, gw), lambda b, g, i: (b * nq + i, g)),
        out_shape=jax.ShapeDtypeStruct((batch * sq, N_Q_HEADS * HEAD_DIM), BF16),
        compiler_params=_cparams(("parallel", "parallel", "arbitrary")),
        name="gqa_attention",
    )(q, k, v)


def _oproj_kernel(x_ref, mod_ref, a_ref, w_ref, o_ref):
    y = jnp.dot(a_ref[...], w_ref[...], preferred_element_type=F32)
    o_ref[...] = x_ref[...] + mod_ref[...] * y


def _attn_out(x, gate, a, w_o):
    m, d = x.shape
    tm = ATTN_OUT_TM
    return pl.pallas_call(
        _oproj_kernel,
        grid=(m // tm,),
        in_specs=[
            pl.BlockSpec((tm, d), lambda i: (i, 0)),
            pl.BlockSpec((None, 1, d), _mod_index(tm)),
            pl.BlockSpec((tm, a.shape[1]), lambda i: (i, 0)),
            pl.BlockSpec(w_o.shape, lambda i: (0, 0)),
        ],
        out_specs=pl.BlockSpec((tm, d), lambda i: (i, 0)),
        out_shape=jax.ShapeDtypeStruct((m, d), F32),
        compiler_params=_cparams(("parallel",)),
        name="attn_out",
    )(x, gate, a, w_o)


def _permute_kernel(w_ref, o_ref):
    w = w_ref[...]
    pad = jnp.zeros((w.shape[0], AB_PAD_COLS - 6192), F32)
    out = jnp.concatenate([w[:, :3072], w[:, 3088:6160], w[:, 3072:3088], w[:, 6160:6192], pad], axis=1)
    o_ref[...] = out.astype(BF16)


def _permute_w_in(w):
    d, n = w.shape
    tr = 256
    return pl.pallas_call(
        _permute_kernel,
        grid=(d // tr,),
        in_specs=[pl.BlockSpec((tr, n), lambda i: (i, 0))],
        out_specs=pl.BlockSpec((tr, AB_PAD_COLS), lambda i: (i, 0)),
        out_shape=jax.ShapeDtypeStruct((d, AB_PAD_COLS), BF16),
        compiler_params=_cparams(("parallel",)),
        name="w_in_layout",
    )(w)


def _rope_tables(n_tokens):
    t = jnp.arange(n_tokens)
    row = (t // GRID_W).astype(F32)
    col = (t % GRID_W).astype(F32)
    inv = ROPE_THETA ** (-jnp.arange(0, ROPE_AXIS, 2, dtype=F32) / ROPE_AXIS)
    ar = row[:, None] * inv
    ac = col[:, None] * inv
    cos = jnp.concatenate([jnp.cos(ar), jnp.cos(ar), jnp.cos(ac), jnp.cos(ac)], axis=1)
    sin = jnp.concatenate([-jnp.sin(ar), jnp.sin(ar), -jnp.sin(ac), jnp.sin(ac)], axis=1)
    return cos, sin


def kernel(x_prompt, x_sample, c, c_ctx, state_mlstm_C, state_mlstm_n, state_mlstm_m, state_gla_S,
           cache_k, cache_v, w_mod, b_mod, norm_g, ffn_w_gate, ffn_w_up, ffn_w_down, w_in_ab,
           mlstm_conv_w, mlstm_conv_b, mlstm_b_i, mlstm_b_f, mlstm_out_g, gla_w_gk, gla_b_gk, gla_out_g,
           w_out_ab, w_qkv, q_norm_g, k_norm_g, w_o):
    bp, sp, d = x_prompt.shape
    bs, ss, _ = x_sample.shape
    depth = w_mod.shape[0]
    xp = x_prompt.reshape(bp * sp, d)
    xs = x_sample.reshape(bs * ss, d)

    c8 = jnp.concatenate([c_ctx[None, :], c, jnp.zeros((8 - 1 - bs, d), F32)], axis=0)
    mod = _modulation(c8, w_mod, b_mod).reshape(depth, 8, N_MOD, d)

    ffn_w = (ffn_w_gate.astype(BF16), ffn_w_up.astype(BF16), ffn_w_down.astype(BF16))
    new_c, new_n, new_m, new_s, new_k, new_v = [], [], [], [], [], []
    for l in range(depth):
        mod_p = mod[l, 0:1]
        mod_s = mod[l, 1:1 + bs]

        def ffn_half(x, md, j, half):
            return _ffn(x, md[:, 3 * j:3 * j + 3], norm_g[l, j][None, :], *ffn_w, l, half)

        xp = ffn_half(xp, mod_p, 0, 0)
        xs = ffn_half(xs, mod_s, 0, 0)
        g_mix = norm_g[l, 1][None, :]
        if l % 2 == 0:
            e = l // 2
            w_in = _permute_w_in(w_in_ab[e])
            w_out = w_out_ab[e].astype(BF16)
            gbias = jnp.zeros((2, 2, HEADS), F32)
            gbias = gbias.at[:, 0].set(mlstm_b_i[e]).at[:, 1].set(mlstm_b_f[e])
            gbias = jnp.pad(gbias.reshape(1, 16), ((0, 0), (0, 128 - 16)))
            wgk = jnp.zeros((2, 128, HEADS * DQK), F32)
            for j in range(2):
                lo = LR_OFF + j * GLA_RANK
                wgk = wgk.at[j, lo:lo + GLA_RANK].set(gla_w_gk[e, j])
            wgk = wgk.astype(BF16)
            bgk = gla_b_gk[e][:, None, :]
            conv_w = mlstm_conv_w[e]
            conv_b = mlstm_conv_b[e][None, :]
            a_gain = mlstm_out_g[e][None, :]
            b_gain = gla_out_g[e][None, :]

            def mixer(x, md, batch, seq, c0, n0, m0, s0, emit_state):
                proj = _proj(x, md[:, 3:6], g_mix, w_in, 1280)
                m0r = jnp.broadcast_to(m0[..., None], m0.shape + (128,))
                r = _mixer_scan(proj, conv_w, conv_b, gbias, wgk, bgk, c0, n0, m0r, s0,
                                batch=batch, seq=seq, emit_state=emit_state)
                y = _mixer_out(x, md[:, 5:6], r[0], r[1], r[2], r[3], proj, a_gain, b_gain, w_out)
                return y, r[4:]

            zc = jnp.zeros((bp, 2, HEADS, DQK, DV), F32)
            zn = jnp.zeros((bp, 2, HEADS, DQK), F32)
            zm = jnp.zeros((bp, 2, HEADS), F32)
            xp, st = mixer(xp, mod_p, bp, sp, zc, zn, zm, zc, True)
            xs, _ = mixer(xs, mod_s, bs, ss, state_mlstm_C[:, e], state_mlstm_n[:, e],
                          state_mlstm_m[:, e], state_gla_S[:, e], False)
            new_c.append(st[0])
            new_n.append(st[1])
            new_m.append(st[2][..., 0])
            new_s.append(st[3])
        else:
            o = l // 2
            wq = w_qkv[o].astype(BF16)
            wo = w_o[o].astype(BF16)
            qg = q_norm_g[o][None, :]
            kg = k_norm_g[o][None, :]
            nk = HEADS * HEAD_DIM
            q, k, v, kf, vf = _qkv_proj(xp, mod_p[:, 3:6], g_mix, wq, qg, kg, None)
            a = _attention(q, k, v, batch=bp, sq=sp, sk=sp)
            xp = _attn_out(xp, mod_p[:, 5:6], a, wo)
            new_k.append(kf.reshape(bp, sp, HEADS, HEAD_DIM))
            new_v.append(vf.reshape(bp, sp, HEADS, HEAD_DIM))
            q, k, v = _qkv_proj(xs, mod_s[:, 3:6], g_mix, wq, qg, kg, _rope_tables(ss))
            past = cache_k.shape[2]
            ck = cache_k[:, o].reshape(bs, past, nk).astype(BF16)
            cv = cache_v[:, o].reshape(bs, past, nk).astype(BF16)
            k_all = jnp.concatenate([ck, k.reshape(bs, ss, nk)], axis=1).reshape(bs * (past + ss), nk)
            v_all = jnp.concatenate([cv, v.reshape(bs, ss, nk)], axis=1).reshape(bs * (past + ss), nk)
            a = _attention(q, k_all, v_all, batch=bs, sq=ss, sk=past + ss)
            xs = _attn_out(xs, mod_s[:, 5:6], a, wo)
        xp = ffn_half(xp, mod_p, 2, 1)
        xs = ffn_half(xs, mod_s, 2, 1)

    return (xp.reshape(bp, sp, d), xs.reshape(bs, ss, d),
            jnp.stack(new_c, axis=1), jnp.stack(new_n, axis=1), jnp.stack(new_m, axis=1),
            jnp.stack(new_s, axis=1), jnp.stack(new_k, axis=1), jnp.stack(new_v, axis=1))
```

```python
import functools

import jax
import jax.numpy as jnp
from jax import lax
from jax.experimental import pallas as pl
from jax.experimental.pallas import tpu as pltpu

F32 = jnp.float32
BF16 = jnp.bfloat16

D_MODEL = 2048
N_MOD = 9
D_FF = 5632
RMS_EPS = 1e-6
ROWS_PER_MOD = 4096
HEADS = 4
DQK = 128
DV = 256
GLA_RANK = 16
GLA_GATE_NORM = 16.0
CONV_W = 3
N_Q_HEADS = 16
Q_PER_KV = 4
HEAD_DIM = 128
GRID_W = 64
ROPE_AXIS = HEAD_DIM // 2
ROPE_THETA = 10000.0

COL_AQ, COL_AK, COL_AV, COL_AO = 0, 512, 1024, 2048
COL_BQ, COL_BK, COL_BV, COL_BG = 3072, 3584, 4096, 5120
COL_SMALL = 6144
AB_PAD_COLS = 6400
LR_OFF = 16

VMEM_LIMIT = 60 * 1024 * 1024
SCAN_L = 128
SCAN_NB = 2
FFN_TM, FFN_TF = 1024, 512
PROJ_TM = 1024
ATTN_OUT_TM = 512
OUT_TM = 256
ATT_TQ = 128
ATT_KVG = 2
ATT_SKEW = 1
ATT_KC = 256
LOG2E = 1.4426950408889634


def _cparams(sem):
    return pltpu.CompilerParams(dimension_semantics=sem, vmem_limit_bytes=VMEM_LIMIT)


def _log_sigmoid(x):
    return jnp.minimum(x, 0.0) - jnp.log1p(jnp.exp(-jnp.abs(x)))


def _adaln(x, g, shift, scale):
    ms = jnp.mean(x * x, axis=-1, keepdims=True)
    return (x * lax.rsqrt(ms + RMS_EPS) * g) * (1.0 + scale) + shift


def _mod_index(tm):
    return lambda i, *_: ((i * tm) // ROWS_PER_MOD, 0, 0)


def _mod_kernel(c_ref, w_ref, b_ref, o_ref):
    c = c_ref[...]
    s = (c * jax.nn.sigmoid(c)).astype(BF16)
    o_ref[...] = jnp.dot(s, w_ref[...].astype(BF16), preferred_element_type=F32) + b_ref[...]


def _modulation(c8, w_mod, b_mod):
    depth, d, n = w_mod.shape
    tn = 1024
    return pl.pallas_call(
        _mod_kernel,
        grid=(depth, n // tn),
        in_specs=[
            pl.BlockSpec((8, d), lambda l, j: (0, 0)),
            pl.BlockSpec((None, d, tn), lambda l, j: (l, 0, j)),
            pl.BlockSpec((None, 1, tn), lambda l, j: (l, 0, j)),
        ],
        out_specs=pl.BlockSpec((None, 8, tn), lambda l, j: (l, 0, j)),
        out_shape=jax.ShapeDtypeStruct((depth, 8, n), F32),
        compiler_params=_cparams(("arbitrary", "arbitrary")),
        name="modulation",
    )(c8, w_mod, b_mod.reshape(depth, 1, n))


def _ffn_kernel(x_ref, mod_ref, g_ref, wg_ref, wu_ref, wd_ref, o_ref, h_ref, *, nf):
    f = pl.program_id(1)

    def swiglu_part(h):
        gate = jnp.dot(h, wg_ref[...], preferred_element_type=F32)
        up = jnp.dot(h, wu_ref[...], preferred_element_type=F32)
        a = (gate * jax.nn.sigmoid(gate) * up).astype(BF16)
        return jnp.dot(a, wd_ref[...], preferred_element_type=F32)

    @pl.when(f == 0)
    def _():
        h = _adaln(x_ref[...], g_ref[...], mod_ref[0:1, :], mod_ref[1:2, :]).astype(BF16)
        h_ref[...] = h
        o_ref[...] = swiglu_part(h)

    @pl.when(jnp.logical_and(f > 0, f < nf - 1))
    def _():
        o_ref[...] += swiglu_part(h_ref[...])

    @pl.when(f == nf - 1)
    def _():
        acc = o_ref[...] + swiglu_part(h_ref[...])
        o_ref[...] = x_ref[...] + (0.5 * mod_ref[2:3, :]) * acc


def _ffn(x, mod3, g, wg, wu, wd, layer, half):
    m, d = x.shape
    tm, tf = FFN_TM, FFN_TF
    nf = D_FF // tf
    return pl.pallas_call(
        functools.partial(_ffn_kernel, nf=nf),
        grid=(m // tm, nf),
        in_specs=[
            pl.BlockSpec((tm, d), lambda i, f: (i, 0)),
            pl.BlockSpec((None, 3, d), _mod_index(tm)),
            pl.BlockSpec((1, d), lambda i, f: (0, 0)),
            pl.BlockSpec((None, None, d, tf), lambda i, f: (layer, half, 0, f)),
            pl.BlockSpec((None, None, d, tf), lambda i, f: (layer, half, 0, f)),
            pl.BlockSpec((None, None, tf, d), lambda i, f: (layer, half, f, 0)),
        ],
        out_specs=pl.BlockSpec((tm, d), lambda i, f: (i, 0)),
        out_shape=jax.ShapeDtypeStruct((m, d), F32),
        scratch_shapes=[pltpu.VMEM((tm, d), BF16)],
        compiler_params=_cparams(("parallel", "arbitrary")),
        name="ffn",
    )(x, mod3, g, wg, wu, wd)


def _proj_kernel(x_ref, mod_ref, g_ref, w_ref, o_ref, h_ref):
    j = pl.program_id(1)

    @pl.when(j == 0)
    def _():
        h = _adaln(x_ref[...], g_ref[...], mod_ref[0:1, :], mod_ref[1:2, :]).astype(BF16)
        h_ref[...] = h
        o_ref[...] = _dot_nt(h, w_ref[...])

    @pl.when(j > 0)
    def _():
        o_ref[...] = _dot_nt(h_ref[...], w_ref[...])


def _proj(x, mod3, g, w, tn):
    m, d = x.shape
    n = w.shape[0]
    tm = PROJ_TM
    return pl.pallas_call(
        _proj_kernel,
        grid=(m // tm, n // tn),
        in_specs=[
            pl.BlockSpec((tm, d), lambda i, j: (i, 0)),
            pl.BlockSpec((None, 3, d), _mod_index(tm)),
            pl.BlockSpec((1, d), lambda i, j: (0, 0)),
            pl.BlockSpec((tn, d), lambda i, j: (j, 0)),
        ],
        out_specs=pl.BlockSpec((tm, tn), lambda i, j: (i, j)),
        out_shape=jax.ShapeDtypeStruct((m, n), F32),
        scratch_shapes=[pltpu.VMEM((tm, d), BF16)],
        compiler_params=_cparams(("parallel", "arbitrary")),
        name="adaln_proj",
    )(x, mod3, g, w)


def _conv_qk_tile(x_ref, prev_ref, next_ref, w_ref, b_ref, first, last):
    x = x_ref[...]
    n = x.shape[0]
    prev_row = jnp.where(first, 0.0, prev_ref[7:8, :])
    next_row = jnp.where(last, 0.0, next_ref[0:1, :])
    ridx = lax.broadcasted_iota(jnp.int32, x.shape, 0)
    xm = jnp.where(ridx == 0, prev_row, pltpu.roll(x, 1, 0))
    xp = jnp.where(ridx == n - 1, next_row, pltpu.roll(x, n - 1, 0))
    y = xm * w_ref[0:1, :] + x * w_ref[1:2, :] + xp * w_ref[2:3, :] + b_ref[...]
    y = y * jax.nn.sigmoid(y)
    hqk = HEADS * DQK
    return y[:, :hqk].astype(BF16), (y[:, hqk:] * (DQK ** -0.5)).astype(BF16)


def _order_mask(n, rev):
    t = lax.broadcasted_iota(jnp.int32, (n, n), 0)
    s = lax.broadcasted_iota(jnp.int32, (n, n), 1)
    return (s >= t) if rev else (s <= t)


def _masked_sum(mask, x):
    n = x.shape[1]
    hi = x.astype(BF16)
    r = x - hi.astype(F32)
    mid = r.astype(BF16)
    lo = (r - mid.astype(F32)).astype(BF16)
    y = jnp.dot(mask.astype(BF16), jnp.concatenate([hi, mid, lo], axis=1), preferred_element_type=F32)
    return y[:, :n] + y[:, n:2 * n] + y[:, 2 * n:]


def _col_rep(row):
    return jnp.broadcast_to(row, (128, 128)).T


def _dot_nt(a, b):
    return lax.dot_general(a, b, (((1,), (1,)), ((), ())), preferred_element_type=F32)


def _dot_tn(a, b):
    return lax.dot_general(a, b, (((0,), (0,)), ((), ())), preferred_element_type=F32)


def _wide(x):
    return jnp.concatenate([x, x], axis=1)


def _mlstm_chain(q, k, v_ref, g_ref, gb_ref, h_ref, c_ref, n_ref, m_ref, *, direction):
    L = SCAN_L
    rev = direction == 1
    mask = _order_mask(L, rev)
    gates = g_ref[...] + gb_ref[...]
    logf = _log_sigmoid(gates)
    bsum = _masked_sum(mask, logf)
    col_i = [8 * direction + h for h in range(HEADS)]
    col_f = [8 * direction + HEADS + h for h in range(HEADS)]
    gates_t = gates.T
    bsum_t = bsum.T
    end = 0 if rev else L - 1

    def head(h):
        hs = slice(h * DQK, (h + 1) * DQK)
        vs = slice(h * DV, (h + 1) * DV)
        b_r = bsum_t[col_f[h]:col_f[h] + 1, :]
        i_r = gates_t[col_i[h]:col_i[h] + 1, :]
        b_c = jnp.broadcast_to(bsum[:, col_f[h]:col_f[h] + 1], (L, 128))
        i_c = jnp.broadcast_to(gates[:, col_i[h]:col_i[h] + 1], (L, 128))
        m_prev = m_ref[direction, h:h + 1, :]
        n_prev = n_ref[direction, h:h + 1, :]
        dmat = jnp.where(mask, b_c - b_r + i_r, -jnp.inf)
        inter = b_c + m_prev
        m_t = jnp.maximum(inter, jnp.max(dmat, axis=-1, keepdims=True))
        w_intra = jnp.exp(dmat - m_t)
        w_inter = jnp.exp(inter - m_t)
        qh = q[:, hs]
        kh = k[:, hs]
        vh = v_ref[:, vs].astype(BF16)
        s = _dot_nt(qh, kh) * w_intra
        c_old = c_ref[direction, h]
        num = _wide(w_inter) * jnp.dot(qh, c_old.astype(BF16), preferred_element_type=F32)
        num = num + jnp.dot(s.astype(BF16), vh, preferred_element_type=F32)
        qn = jnp.sum(qh.astype(F32) * n_prev, axis=-1, keepdims=True)
        den = w_inter * qn + jnp.sum(s, axis=-1, keepdims=True)
        inv = 1.0 / jnp.maximum(jnp.abs(den), jnp.exp(-m_t))
        h_ref[:, vs] = num * _wide(inv)

        g_end = b_c[end:end + 1, :]
        dec = g_end - b_c + i_c
        m_new = jnp.maximum(g_end + m_prev, jnp.max(dec, axis=0, keepdims=True))
        ws = jnp.exp(dec - m_new)
        wc = jnp.exp(g_end + m_prev - m_new)
        kw = kh.astype(F32) * ws
        c_ref[direction, h] = _wide(wc) * c_old + _dot_tn(kw.astype(BF16), vh)
        n_ref[direction, h:h + 1, :] = wc * n_prev + jnp.sum(kw, axis=0, keepdims=True)
        m_ref[direction, h:h + 1, :] = m_new

    return head


def _gla_chain(q_ref, k_ref, v_ref, g_ref, wgk_ref, bgk_ref, o_ref, s_ref, *, direction):
    L = SCAN_L
    H = L // 2
    rev = direction == 1
    glogit = jnp.dot(g_ref[...].astype(BF16), wgk_ref[direction], preferred_element_type=F32)
    la = _log_sigmoid(glogit + bgk_ref[direction]) * (1.0 / GLA_GATE_NORM)
    bc = _masked_sum(_order_mask(L, rev), la)
    mask_h = _order_mask(H, rev)
    first = slice(H, L) if rev else slice(0, H)
    second = slice(0, H) if rev else slice(H, L)
    first_end = H if rev else H - 1
    end = 0 if rev else L - 1
    scale = DQK ** -0.5

    def scores(qx, kx, bq, bk, anchor):
        qt = (qx * jnp.exp(bq - anchor)).astype(BF16)
        kt = (kx * jnp.exp(anchor - bk)).astype(BF16)
        return _dot_nt(qt, kt)

    def head(h):
        hs = slice(h * DQK, (h + 1) * DQK)
        vs = slice(h * DV, (h + 1) * DV)
        bch = bc[:, hs]
        qh = q_ref[:, hs] * scale
        kh = k_ref[:, hs]
        vh = v_ref[:, vs].astype(BF16)
        s_old = s_ref[direction, h]
        o = jnp.dot((qh * jnp.exp(bch)).astype(BF16), s_old.astype(BF16), preferred_element_type=F32)

        def diag(rows):
            mid = rows.start + H // 2
            a = scores(qh[rows], kh[rows], bch[rows], bch[rows], bch[mid:mid + 1, :])
            return jnp.where(mask_h, a, 0.0).astype(BF16)

        a_ff = diag(first)
        a_ss = diag(second)
        a_sf = scores(qh[second], kh[first], bch[second], bch[first],
                      bch[first_end:first_end + 1, :]).astype(BF16)
        o_ref[first, vs] = o[first] + jnp.dot(a_ff, vh[first], preferred_element_type=F32)
        o_ref[second, vs] = (o[second] + jnp.dot(a_sf, vh[first], preferred_element_type=F32)
                             + jnp.dot(a_ss, vh[second], preferred_element_type=F32))

        b_end = bch[end:end + 1, :]
        kdec = (kh * jnp.exp(b_end - bch)).astype(BF16)
        carry = _col_rep(jnp.exp(b_end))
        s_ref[direction, h] = _wide(carry) * s_old + _dot_tn(kdec, vh)

    return head


def _scan_kernel(*refs, nc, emit_state):
    per_dir = [refs[0:8], refs[8:16]]
    cw_ref, cb_ref, gb_ref, wgk_ref, bgk_ref, c0_ref, n0_ref, m0_ref, s0_ref = refs[16:25]
    hm_refs = refs[25:27]
    hg_refs = refs[27:29]
    rest = refs[29:]
    if emit_state:
        cf_ref, nf_ref, mf_ref, sf_ref = rest[:4]
        rest = rest[4:]
    c_ref, n_ref, m_ref, s_ref = rest
    c = pl.program_id(1)

    @pl.when(c == 0)
    def _():
        c_ref[...] = c0_ref[...]
        n_ref[...] = n0_ref[...]
        m_ref[...] = m0_ref[...]
        s_ref[...] = s0_ref[...]

    chains = []
    for e in range(SCAN_NB):
        for direction in range(2):
            aqk, prev, nxt, mv, small, gq, gk, gv = (r.at[e] for r in per_dir[direction])
            pos = nc - 1 - c if direction == 1 else c
            mq, mk = _conv_qk_tile(aqk, prev, nxt, cw_ref, cb_ref, pos == 0, pos == nc - 1)
            chains.append(_mlstm_chain(mq, mk, mv, small, gb_ref, hm_refs[direction].at[e],
                                       c_ref.at[e], n_ref.at[e], m_ref.at[e], direction=direction))
            chains.append(_gla_chain(gq, gk, gv, small, wgk_ref, bgk_ref, hg_refs[direction].at[e],
                                     s_ref.at[e], direction=direction))
    for h in range(HEADS):
        for chain_head in chains:
            chain_head(h)

    if emit_state:
        @pl.when(c == nc - 1)
        def _():
            cf_ref[...] = c_ref[...]
            nf_ref[...] = n_ref[...]
            mf_ref[...] = m_ref[...]
            sf_ref[...] = s_ref[...]


def _mixer_scan(proj, conv_w, conv_b, gbias, wgk, bgk, c0, n0, m0, s0, *, batch, seq, emit_state):
    L = SCAN_L
    nb = SCAN_NB
    nc = seq // L
    qk_w, v_w = HEADS * DQK, HEADS * DV
    proj = proj.reshape(batch, seq, proj.shape[1])
    l8, last8 = L // 8, seq // 8 - 1

    def dir_specs(rev):
        chunk = (lambda c: nc - 1 - c) if rev else (lambda c: c)
        spec = lambda w, col: pl.BlockSpec((nb, L, w), lambda b, c: (b, chunk(c), col))
        prev = pl.BlockSpec((nb, 8, 2 * qk_w), lambda b, c: (b, jnp.maximum(chunk(c) * l8 - 1, 0), 0))
        nxt = pl.BlockSpec((nb, 8, 2 * qk_w), lambda b, c: (b, jnp.minimum((chunk(c) + 1) * l8, last8), 0))
        ins = [spec(2 * qk_w, 0), prev, nxt, spec(v_w, COL_AV // v_w), spec(128, COL_SMALL // 128),
               spec(qk_w, COL_BQ // qk_w), spec(qk_w, COL_BK // qk_w), spec(v_w, COL_BV // v_w)]
        return ins, spec(v_w, 0)

    ins_f, out_f = dir_specs(False)
    ins_r, out_r = dir_specs(True)
    state = lambda *tail: pl.BlockSpec((nb, 2, HEADS) + tail, lambda b, c: (b, 0, 0) + (0,) * len(tail))
    in_specs = ins_f + ins_r + [
        pl.BlockSpec((CONV_W, 2 * qk_w), lambda b, c: (0, 0)),
        pl.BlockSpec((1, 2 * qk_w), lambda b, c: (0, 0)),
        pl.BlockSpec((1, 128), lambda b, c: (0, 0)),
        pl.BlockSpec((2, 128, qk_w), lambda b, c: (0, 0, 0)),
        pl.BlockSpec((2, 1, qk_w), lambda b, c: (0, 0, 0)),
        state(DQK, DV), state(DQK), state(128), state(DQK, DV),
    ]
    out_specs = [out_f, out_r, out_f, out_r]
    out_shape = [jax.ShapeDtypeStruct((batch, seq, v_w), F32)] * 4
    state_shapes = [(batch, 2, HEADS, DQK, DV), (batch, 2, HEADS, DQK), (batch, 2, HEADS, 128),
                    (batch, 2, HEADS, DQK, DV)]
    if emit_state:
        out_specs += [state(DQK, DV), state(DQK), state(128), state(DQK, DV)]
        out_shape += [jax.ShapeDtypeStruct(sh, F32) for sh in state_shapes]
    per_dir_args = [proj] * 8
    outs = pl.pallas_call(
        functools.partial(_scan_kernel, nc=nc, emit_state=emit_state),
        grid=(batch // nb, nc),
        in_specs=in_specs,
        out_specs=out_specs,
        out_shape=out_shape,
        scratch_shapes=[pltpu.VMEM((nb,) + sh[1:], F32) for sh in state_shapes],
        compiler_params=_cparams(("parallel", "arbitrary")),
        name="mixer_scan",
    )(*per_dir_args, *per_dir_args, conv_w, conv_b, gbias, wgk, bgk, c0, n0, m0, s0)
    return [o.reshape(batch * seq, v_w) for o in outs[:4]] + list(outs[4:])


def _head_rms(x):
    parts = []
    for h in range(HEADS):
        xh = x[:, h * DV:(h + 1) * DV]
        ms = jnp.mean(xh * xh, axis=-1, keepdims=True)
        parts.append(xh * lax.rsqrt(ms + RMS_EPS))
    return jnp.concatenate(parts, axis=1)


def _mixout_kernel(x_ref, mod_ref, hmf_ref, hmr_ref, hgf_ref, hgr_ref, ao_ref, bg_ref,
                   ag_ref, bgn_ref, w_ref, o_ref):
    ya = _head_rms(hmf_ref[...] + hmr_ref[...]) * ag_ref[...] * jax.nn.sigmoid(ao_ref[...])
    bg = bg_ref[...]
    yb = _head_rms(hgf_ref[...] + hgr_ref[...]) * bgn_ref[...] * (bg * jax.nn.sigmoid(bg))
    hw = HEADS * DV
    y = jnp.dot(ya.astype(BF16), w_ref[0:hw, :], preferred_element_type=F32)
    y = y + jnp.dot(yb.astype(BF16), w_ref[hw:2 * hw, :], preferred_element_type=F32)
    o_ref[...] = x_ref[...] + mod_ref[...] * y


def _mixer_out(x, gate, hmf, hmr, hgf, hgr, proj, a_gain, b_gain, w_out):
    m, d = x.shape
    tm = OUT_TM
    hw = HEADS * DV
    wide = lambda col: pl.BlockSpec((tm, hw), lambda i: (i, col))
    return pl.pallas_call(
        _mixout_kernel,
        grid=(m // tm,),
        in_specs=[
            pl.BlockSpec((tm, d), lambda i: (i, 0)),
            pl.BlockSpec((None, 1, d), _mod_index(tm)),
            wide(0), wide(0), wide(0), wide(0),
            wide(COL_AO // hw), wide(COL_BG // hw),
            pl.BlockSpec((1, hw), lambda i: (0, 0)),
            pl.BlockSpec((1, hw), lambda i: (0, 0)),
            pl.BlockSpec((2 * hw, d), lambda i: (0, 0)),
        ],
        out_specs=pl.BlockSpec((tm, d), lambda i: (i, 0)),
        out_shape=jax.ShapeDtypeStruct((m, d), F32),
        compiler_params=_cparams(("parallel",)),
        name="mixer_out",
    )(x, gate, hmf, hmr, hgf, hgr, proj, proj, a_gain, b_gain, w_out)


def _rope(x, cos, sin):
    n = x.shape[1]
    lane = lax.broadcasted_iota(jnp.int32, x.shape, 1)
    partner = jnp.where((lane % 64) < 32, pltpu.roll(x, n - 32, 1), pltpu.roll(x, 32, 1))
    reps = n // HEAD_DIM
    cos_t = jnp.concatenate([cos] * reps, axis=1) if reps > 1 else cos
    sin_t = jnp.concatenate([sin] * reps, axis=1) if reps > 1 else sin
    return x * cos_t + partner * sin_t


def _head_norm(x, g, n_heads):
    parts = []
    for h in range(n_heads):
        xh = x[:, h * HEAD_DIM:(h + 1) * HEAD_DIM]
        ms = jnp.mean(xh * xh, axis=-1, keepdims=True)
        parts.append(xh * lax.rsqrt(ms + RMS_EPS) * g)
    return jnp.concatenate(parts, axis=1)


def _qkv_kernel(x_ref, mod_ref, g_ref, w_ref, qg_ref, kg_ref, *rest, rope):
    if rope:
        cos_ref, sin_ref, q_ref, k_ref, v_ref, h_ref = rest
    else:
        q_ref, k_ref, v_ref, kf_ref, vf_ref, h_ref = rest
    j = pl.program_id(1)
    nk = HEADS * HEAD_DIM

    def finish_q(y):
        q = _head_norm(y, qg_ref[...], y.shape[1] // HEAD_DIM)
        if rope:
            q = _rope(q, cos_ref[...], sin_ref[...])
        q_ref[...] = (q * (HEAD_DIM ** -0.5 * LOG2E)).astype(BF16)

    def finish_kv(y):
        k = _head_norm(y[:, :nk], kg_ref[...], HEADS)
        v = y[:, nk:]
        if rope:
            k = _rope(k, cos_ref[...], sin_ref[...])
        else:
            kf_ref[...] = k
            vf_ref[...] = v
        k_ref[...] = k.astype(BF16)
        v_ref[...] = v.astype(BF16)

    @pl.when(j == 0)
    def _():
        h = _adaln(x_ref[...], g_ref[...], mod_ref[0:1, :], mod_ref[1:2, :]).astype(BF16)
        h_ref[...] = h
        finish_q(jnp.dot(h, w_ref[...], preferred_element_type=F32))

    @pl.when(j == 1)
    def _():
        finish_q(jnp.dot(h_ref[...], w_ref[...], preferred_element_type=F32))

    @pl.when(j == 2)
    def _():
        finish_kv(jnp.dot(h_ref[...], w_ref[...], preferred_element_type=F32))


def _qkv_proj(x, mod3, g, w, qg, kg, rope_tabs):
    m, d = x.shape
    tm = PROJ_TM
    nq = N_Q_HEADS * HEAD_DIM
    nk = HEADS * HEAD_DIM
    tn = 2 * nk
    rope = rope_tabs is not None
    in_specs = [
        pl.BlockSpec((tm, d), lambda i, j: (i, 0)),
        pl.BlockSpec((None, 3, d), _mod_index(tm)),
        pl.BlockSpec((1, d), lambda i, j: (0, 0)),
        pl.BlockSpec((d, tn), lambda i, j: (0, j)),
        pl.BlockSpec((1, HEAD_DIM), lambda i, j: (0, 0)),
        pl.BlockSpec((1, HEAD_DIM), lambda i, j: (0, 0)),
    ]
    args = [x, mod3, g, w, qg, kg]
    narrow = pl.BlockSpec((tm, nk), lambda i, j: (i, 0))
    out_specs = [pl.BlockSpec((tm, tn), lambda i, j: (i, jnp.minimum(j, nq // tn - 1))), narrow, narrow]
    out_shape = [
        jax.ShapeDtypeStruct((m, nq), BF16),
        jax.ShapeDtypeStruct((m, nk), BF16),
        jax.ShapeDtypeStruct((m, nk), BF16),
    ]
    if rope:
        nt = rope_tabs[0].shape[0] // tm
        in_specs += [pl.BlockSpec((tm, HEAD_DIM), lambda i, j: (i % nt, 0))] * 2
        args += list(rope_tabs)
    else:
        out_specs += [narrow, narrow]
        out_shape += [jax.ShapeDtypeStruct((m, nk), F32)] * 2
    return pl.pallas_call(
        functools.partial(_qkv_kernel, rope=rope),
        grid=(m // tm, (nq + 2 * nk) // tn),
        in_specs=in_specs,
        out_specs=out_specs,
        out_shape=out_shape,
        scratch_shapes=[pltpu.VMEM((tm, d), BF16)],
        compiler_params=_cparams(("parallel", "arbitrary")),
        name="qkv_proj",
    )(*args)


def _attn_kernel(q_ref, k_ref, v_ref, o_ref, sc_ref, *, tq, sk):
    chunks = [slice(lo, min(lo + ATT_KC, sk)) for lo in range(0, sk, ATT_KC)]
    n_chains = ATT_KVG * Q_PER_KV
    head = lambda j: slice(j * HEAD_DIM, (j + 1) * HEAD_DIM)
    kv = lambda j: head(j // Q_PER_KV)

    row_max = [None] * n_chains

    def scores_pass(j):
        qj = q_ref[:, head(j)]
        mrun = None
        for ch in chunks:
            s = _dot_nt(qj, k_ref[ch, kv(j)])
            sc_ref[j, :, ch] = s
            for b in range(s.shape[1] // 128):
                sb = s[:, b * 128:(b + 1) * 128]
                mrun = sb if mrun is None else jnp.maximum(mrun, sb)
            yield
        row_max[j] = jnp.max(mrun, axis=-1, keepdims=True)

    def values_pass(j):
        m = row_max[j]
        lrun = jnp.zeros((tq, 128), F32)
        acc = jnp.zeros((tq, HEAD_DIM), F32)
        for ch in chunks:
            p = jnp.exp2(sc_ref[j, :, ch] - m)
            for b in range(p.shape[1] // 128):
                lrun = lrun + p[:, b * 128:(b + 1) * 128]
            acc = acc + jnp.dot(p.astype(BF16), v_ref[ch, kv(j)], preferred_element_type=F32)
            yield
        o = acc * (1.0 / jnp.sum(lrun, axis=-1, keepdims=True))
        o_ref[:, head(j)] = o.astype(BF16)

    order = [scores_pass(j) for j in range(ATT_SKEW)]
    for j in range(ATT_SKEW, n_chains):
        order += [scores_pass(j), values_pass(j - ATT_SKEW)]
    order += [values_pass(j) for j in range(n_chains - ATT_SKEW, n_chains)]
    for task in order:
        for _ in task:
            pass


def _attention(q, k, v, *, batch, sq, sk):
    tq = ATT_TQ
    nq = sq // tq
    gw = ATT_KVG * Q_PER_KV * HEAD_DIM
    kw = ATT_KVG * HEAD_DIM
    return pl.pallas_call(
        functools.partial(_attn_kernel, tq=tq, sk=sk),
        scratch_shapes=[pltpu.VMEM((ATT_KVG * Q_PER_KV, tq, sk), F32)],
        grid=(batch, HEADS // ATT_KVG, nq),
        in_specs=[
            pl.BlockSpec((tq, gw), lambda b, g, i: (b * nq + i, g)),
            pl.BlockSpec((sk, kw), lambda b, g, i: (b, g)),
            pl.BlockSpec((sk, kw), lambda b, g, i: (b, g)),
        ],
        out_specs=pl.BlockSpec((tq, gw), lambda b, g, i: (b * nq + i, g)),
        out_shape=jax.ShapeDtypeStruct((batch * sq, N_Q_HEADS * HEAD_DIM), BF16),
        compiler_params=_cparams(("parallel", "parallel", "arbitrary")),
        name="gqa_attention",
    )(q, k, v)


def _oproj_kernel(x_ref, mod_ref, a_ref, w_ref, o_ref):
    y = jnp.dot(a_ref[...], w_ref[...], preferred_element_type=F32)
    o_ref[...] = x_ref[...] + mod_ref[...] * y


def _attn_out(x, gate, a, w_o):
    m, d = x.shape
    tm = ATTN_OUT_TM
    return pl.pallas_call(
        _oproj_kernel,
        grid=(m // tm,),
        in_specs=[
            pl.BlockSpec((tm, d), lambda i: (i, 0)),
            pl.BlockSpec((None, 1, d), _mod_index(tm)),
            pl.BlockSpec((tm, a.shape[1]), lambda i: (i, 0)),
            pl.BlockSpec(w_o.shape, lambda i: (0, 0)),
        ],
        out_specs=pl.BlockSpec((tm, d), lambda i: (i, 0)),
        out_shape=jax.ShapeDtypeStruct((m, d), F32),
        compiler_params=_cparams(("parallel",)),
        name="attn_out",
    )(x, gate, a, w_o)


def _permute_kernel(w_ref, o_ref):
    w = w_ref[...]
    pad = jnp.zeros((AB_PAD_COLS - 6192, w.shape[1]), F32)
    out = jnp.concatenate([w[:3072], w[3088:6160], w[3072:3088], w[6160:6192], pad], axis=0)
    o_ref[...] = out.astype(BF16)


def _permute_w_in(w):
    d, n = w.shape
    w = jnp.swapaxes(w, 0, 1)
    tr = 256
    return pl.pallas_call(
        _permute_kernel,
        grid=(d // tr,),
        in_specs=[pl.BlockSpec((n, tr), lambda i: (0, i))],
        out_specs=pl.BlockSpec((AB_PAD_COLS, tr), lambda i: (0, i)),
        out_shape=jax.ShapeDtypeStruct((AB_PAD_COLS, d), BF16),
        compiler_params=_cparams(("parallel",)),
        name="w_in_layout",
    )(w)


def _rope_tables(n_tokens):
    t = jnp.arange(n_tokens)
    row = (t // GRID_W).astype(F32)
    col = (t % GRID_W).astype(F32)
    inv = ROPE_THETA ** (-jnp.arange(0, ROPE_AXIS, 2, dtype=F32) / ROPE_AXIS)
    ar = row[:, None] * inv
    ac = col[:, None] * inv
    cos = jnp.concatenate([jnp.cos(ar), jnp.cos(ar), jnp.cos(ac), jnp.cos(ac)], axis=1)
    sin = jnp.concatenate([-jnp.sin(ar), jnp.sin(ar), -jnp.sin(ac), jnp.sin(ac)], axis=1)
    return cos, sin


def kernel(x_prompt, x_sample, c, c_ctx, state_mlstm_C, state_mlstm_n, state_mlstm_m, state_gla_S,
           cache_k, cache_v, w_mod, b_mod, norm_g, ffn_w_gate, ffn_w_up, ffn_w_down, w_in_ab,
           mlstm_conv_w, mlstm_conv_b, mlstm_b_i, mlstm_b_f, mlstm_out_g, gla_w_gk, gla_b_gk, gla_out_g,
           w_out_ab, w_qkv, q_norm_g, k_norm_g, w_o):
    bp, sp, d = x_prompt.shape
    bs, ss, _ = x_sample.shape
    depth = w_mod.shape[0]
    xp = x_prompt.reshape(bp * sp, d)
    xs = x_sample.reshape(bs * ss, d)

    c8 = jnp.concatenate([c_ctx[None, :], c, jnp.zeros((8 - 1 - bs, d), F32)], axis=0)
    mod = _modulation(c8, w_mod, b_mod).reshape(depth, 8, N_MOD, d)

    ffn_w = (ffn_w_gate.astype(BF16), ffn_w_up.astype(BF16), ffn_w_down.astype(BF16))
    new_c, new_n, new_m, new_s, new_k, new_v = [], [], [], [], [], []
    for l in range(depth):
        mod_p = mod[l, 0:1]
        mod_s = mod[l, 1:1 + bs]

        def ffn_half(x, md, j, half):
            return _ffn(x, md[:, 3 * j:3 * j + 3], norm_g[l, j][None, :], *ffn_w, l, half)

        xp = ffn_half(xp, mod_p, 0, 0)
        xs = ffn_half(xs, mod_s, 0, 0)
        g_mix = norm_g[l, 1][None, :]
        if l % 2 == 0:
            e = l // 2
            w_in = _permute_w_in(w_in_ab[e])
            w_out = w_out_ab[e].astype(BF16)
            gbias = jnp.zeros((2, 2, HEADS), F32)
            gbias = gbias.at[:, 0].set(mlstm_b_i[e]).at[:, 1].set(mlstm_b_f[e])
            gbias = jnp.pad(gbias.reshape(1, 16), ((0, 0), (0, 128 - 16)))
            wgk = jnp.zeros((2, 128, HEADS * DQK), F32)
            for j in range(2):
                lo = LR_OFF + j * GLA_RANK
                wgk = wgk.at[j, lo:lo + GLA_RANK].set(gla_w_gk[e, j])
            wgk = wgk.astype(BF16)
            bgk = gla_b_gk[e][:, None, :]
            conv_w = mlstm_conv_w[e]
            conv_b = mlstm_conv_b[e][None, :]
            a_gain = mlstm_out_g[e][None, :]
            b_gain = gla_out_g[e][None, :]

            def mixer(x, md, batch, seq, c0, n0, m0, s0, emit_state):
                proj = _proj(x, md[:, 3:6], g_mix, w_in, 1280)
                m0r = jnp.broadcast_to(m0[..., None], m0.shape + (128,))
                r = _mixer_scan(proj, conv_w, conv_b, gbias, wgk, bgk, c0, n0, m0r, s0,
                                batch=batch, seq=seq, emit_state=emit_state)
                y = _mixer_out(x, md[:, 5:6], r[0], r[1], r[2], r[3], proj, a_gain, b_gain, w_out)
                return y, r[4:]

            zc = jnp.zeros((bp, 2, HEADS, DQK, DV), F32)
            zn = jnp.zeros((bp, 2, HEADS, DQK), F32)
            zm = jnp.zeros((bp, 2, HEADS), F32)
            xp, st = mixer(xp, mod_p, bp, sp, zc, zn, zm, zc, True)
            xs, _ = mixer(xs, mod_s, bs, ss, state_mlstm_C[:, e], state_mlstm_n[:, e],
                          state_mlstm_m[:, e], state_gla_S[:, e], False)
            new_c.append(st[0])
            new_n.append(st[1])
            new_m.append(st[2][..., 0])
            new_s.append(st[3])
        else:
            o = l // 2
            wq = w_qkv[o].astype(BF16)
            wo = w_o[o].astype(BF16)
            qg = q_norm_g[o][None, :]
            kg = k_norm_g[o][None, :]
            nk = HEADS * HEAD_DIM
            q, k, v, kf, vf = _qkv_proj(xp, mod_p[:, 3:6], g_mix, wq, qg, kg, None)
            a = _attention(q, k, v, batch=bp, sq=sp, sk=sp)
            xp = _attn_out(xp, mod_p[:, 5:6], a, wo)
            new_k.append(kf.reshape(bp, sp, HEADS, HEAD_DIM))
            new_v.append(vf.reshape(bp, sp, HEADS, HEAD_DIM))
            q, k, v = _qkv_proj(xs, mod_s[:, 3:6], g_mix, wq, qg, kg, _rope_tables(ss))
            past = cache_k.shape[2]
            ck = cache_k[:, o].reshape(bs, past, nk).astype(BF16)
            cv = cache_v[:, o].reshape(bs, past, nk).astype(BF16)
            k_all = jnp.concatenate([ck, k.reshape(bs, ss, nk)], axis=1).reshape(bs * (past + ss), nk)
            v_all = jnp.concatenate([cv, v.reshape(bs, ss, nk)], axis=1).reshape(bs * (past + ss), nk)
            a = _attention(q, k_all, v_all, batch=bs, sq=ss, sk=past + ss)
            xs = _attn_out(xs, mod_s[:, 5:6], a, wo)
        xp = ffn_half(xp, mod_p, 2, 1)
        xs = ffn_half(xs, mod_s, 2, 1)

    return (xp.reshape(bp, sp, d), xs.reshape(bs, ss, d),
            jnp.stack(new_c, axis=1), jnp.stack(new_n, axis=1), jnp.stack(new_m, axis=1),
            jnp.stack(new_s, axis=1), jnp.stack(new_k, axis=1), jnp.stack(new_v, axis=1))
```
